```python
import jax, jax.numpy as jnp
from jax import lax
import numpy as np

D_MODEL = 4096
BATCH = 1
SEQ = 8192
DEPTH = 2

PLE_DIM = 256
CONV_CH = D_MODEL // 2
CONV_KERNEL = 31
GLA_HEADS = 4
GLA_DK_TOTAL = D_MODEL // 4
GLA_DV_TOTAL = D_MODEL // 2
GLA_HEAD_K = GLA_DK_TOTAL // GLA_HEADS
GLA_HEAD_V = GLA_DV_TOTAL // GLA_HEADS
GLA_GATE_RANK = 16
GLA_GATE_TEMP = 16.0
GLA_CHUNK = 64
N_EXPERTS = 16
N_GROUPS = 4
EXPERTS_PER_GROUP = N_EXPERTS // N_GROUPS
TOP_K = 2
D_FF_EXPERT = 3 * D_MODEL // 8
LN_EPS = 1e-5
DEEPNORM_ALPHA = (2.0 * DEPTH) ** 0.25
DEEPNORM_BETA = (8.0 * DEPTH) ** -0.25
OFF_CONV_A = 0
OFF_CONV_B = OFF_CONV_A + CONV_CH
OFF_Q = OFF_CONV_B + CONV_CH
OFF_K = OFF_Q + GLA_DK_TOTAL
OFF_V = OFF_K + GLA_DK_TOTAL
OFF_R = OFF_V + GLA_DV_TOTAL
OFF_ALPHA = OFF_R + GLA_DV_TOTAL
OFF_GATE = OFF_ALPHA + GLA_GATE_RANK
N_IN = OFF_GATE + 2 * D_MODEL

kernel_name = 'hybrid_conv_gla_groupmoe_deepnorm'


def layer_norm(x, g, b):
    xf = x.astype(jnp.float32)
    mu = jnp.mean(xf, axis=-1, keepdims=True)
    var = jnp.mean(jnp.square(xf - mu), axis=-1, keepdims=True)
    y = (xf - mu) * lax.rsqrt(var + LN_EPS)
    return (y * g.astype(jnp.float32) + b.astype(jnp.float32)).astype(x.dtype)


def causal_depthwise_conv(u, w, b):
    y = lax.conv_general_dilated(
        u, w[:, None, :].astype(u.dtype), window_strides=(1,),
        padding=[(CONV_KERNEL - 1, 0)], dimension_numbers=('NWC', 'WIO', 'NWC'),
        feature_group_count=u.shape[-1])
    return y + b.astype(u.dtype)


def conv_branch(a, g, conv_w, conv_b, ln_g, ln_b, w_o):
    u = a * jax.nn.sigmoid(g)
    u = causal_depthwise_conv(u, conv_w, conv_b)
    u = jax.nn.silu(layer_norm(u, ln_g, ln_b))
    return u @ w_o


def gla_chunked(q, k, v, log_a):
    B, S, H, K = q.shape
    V = v.shape[-1]
    n = S // GLA_CHUNK

    def chunks(t):
        return t.reshape(B, n, GLA_CHUNK, H, t.shape[-1]).transpose(1, 0, 2, 3, 4)

    qc, kc, vc = chunks(q), chunks(k), chunks(v)
    bc = jnp.cumsum(chunks(log_a), axis=2)
    causal = jnp.tril(jnp.ones((GLA_CHUNK, GLA_CHUNK), dtype=bool))[None, :, :, None, None]

    def step(state, inp):
        q_, k_, v_, b_ = inp
        b_last = b_[:, -1]
        inter = jnp.einsum('bchk,bhkv->bchv', q_ * jnp.exp(b_), state)
        decay = jnp.exp(jnp.where(causal, b_[:, :, None] - b_[:, None, :], -jnp.inf))
        scores = jnp.einsum('bihk,bjhk,bijhk->bhij', q_, k_, decay)
        intra = jnp.einsum('bhij,bjhv->bihv', scores, v_)
        k_dec = k_ * jnp.exp(b_last[:, None] - b_)
        state = state * jnp.exp(b_last)[..., None] + jnp.einsum('bchk,bchv->bhkv', k_dec, v_)
        return state, inter + intra

    state0 = jnp.zeros((B, H, K, V), jnp.float32)
    _, out = lax.scan(step, state0, (qc, kc, vc, bc))
    return out.transpose(1, 0, 2, 3, 4).reshape(B, S, H, V)


def gla_branch(q, k, v, r, a_low, w_a_up, b_a, norm_g, w_o):
    B, S, _ = q.shape
    dt = q.dtype
    f32 = jnp.float32
    qh = q.reshape(B, S, GLA_HEADS, GLA_HEAD_K).astype(f32) * (GLA_HEAD_K ** -0.5)
    kh = k.reshape(B, S, GLA_HEADS, GLA_HEAD_K).astype(f32)
    vh = v.reshape(B, S, GLA_HEADS, GLA_HEAD_V).astype(f32)
    log_a = jax.nn.log_sigmoid((a_low @ w_a_up + b_a).astype(f32)) / GLA_GATE_TEMP
    log_a = log_a.reshape(B, S, GLA_HEADS, GLA_HEAD_K)
    o = gla_chunked(qh, kh, vh, log_a)
    o = o * lax.rsqrt(jnp.mean(jnp.square(o), axis=-1, keepdims=True) + LN_EPS)
    o = o * norm_g.astype(f32).reshape(GLA_HEADS, GLA_HEAD_V)
    o = o.reshape(B, S, GLA_DV_TOTAL).astype(dt) * jax.nn.silu(r)
    return o @ w_o


def token_mixer(x, w_in, b_gate, w_a_up, b_a, conv_w, conv_b, conv_ln_g, conv_ln_b,
                w_conv_out, gla_norm_g, w_gla_out, w_out):
    h = x @ w_in
    y_conv = conv_branch(h[..., OFF_CONV_A:OFF_CONV_B], h[..., OFF_CONV_B:OFF_Q],
                         conv_w, conv_b, conv_ln_g, conv_ln_b, w_conv_out)
    y_gla = gla_branch(h[..., OFF_Q:OFF_K], h[..., OFF_K:OFF_V], h[..., OFF_V:OFF_R],
                       h[..., OFF_R:OFF_ALPHA], h[..., OFF_ALPHA:OFF_GATE],
                       w_a_up, b_a, gla_norm_g, w_gla_out)
    gates = jax.nn.sigmoid(h[..., OFF_GATE:] + b_gate)
    merged = gates[..., :D_MODEL] * y_conv + gates[..., D_MODEL:] * y_gla
    return merged @ w_out


def group_moe(x, w_router, b_router, w_gate, w_up, w_down):
    B, S, D = x.shape
    xt = x.reshape(B * S, D)
    logits = (xt @ w_router).astype(jnp.float32) + b_router.astype(jnp.float32)
    scores = jax.nn.softmax(logits, axis=-1)
    grouped = scores.reshape(-1, N_GROUPS, EXPERTS_PER_GROUP)
    group_score = jnp.sum(lax.top_k(grouped, TOP_K)[0], axis=-1)
    best_group = jnp.argmax(group_score, axis=-1)
    in_group = (jnp.arange(N_EXPERTS) // EXPERTS_PER_GROUP)[None, :] == best_group[:, None]
    vals, idx = lax.top_k(jnp.where(in_group, scores, -1.0), TOP_K)
    vals = vals / jnp.sum(vals, axis=-1, keepdims=True)
    combine = jnp.sum(jax.nn.one_hot(idx, N_EXPERTS, dtype=jnp.float32) * vals[..., None], axis=1)
    combine = combine.astype(x.dtype)
    y = jnp.zeros_like(xt)
    for e in range(N_EXPERTS):
        he = jax.nn.silu(xt @ w_gate[e]) * (xt @ w_up[e])
        y = y + combine[:, e:e + 1] * (he @ w_down[e])
    return y.reshape(B, S, D)


def setup_inputs(seed: int = 0) -> dict:
    key = jax.random.key(seed)
    ks = jax.random.split(key, 24)
    nrm = jax.random.normal
    f32 = jnp.float32
    beta = DEEPNORM_BETA
    col_scale = jnp.ones((N_IN,), f32).at[OFF_CONV_A:OFF_CONV_B].set(beta).at[OFF_V:OFF_R].set(beta)
    return {
        'x': nrm(ks[0], (BATCH, SEQ, D_MODEL), f32),
        'p': nrm(ks[1], (DEPTH, BATCH, SEQ, PLE_DIM), f32),
        'w_in': nrm(ks[2], (DEPTH, D_MODEL, N_IN), f32) * (D_MODEL ** -0.5) * col_scale,
        'b_branch_gate': 0.02 * nrm(ks[3], (DEPTH, 2 * D_MODEL), f32),
        'w_gla_gate_up': nrm(ks[4], (DEPTH, GLA_GATE_RANK, GLA_DK_TOTAL), f32) * (GLA_GATE_RANK ** -0.5),
        'b_gla_gate': 1.0 + 0.1 * nrm(ks[5], (DEPTH, GLA_DK_TOTAL), f32),
        'conv_w': nrm(ks[6], (DEPTH, CONV_KERNEL, CONV_CH), f32) * (CONV_KERNEL ** -0.5),
        'conv_b': 0.01 * nrm(ks[7], (DEPTH, CONV_CH), f32),
        'conv_ln_g': 1.0 + 0.01 * nrm(ks[8], (DEPTH, CONV_CH), f32),
        'conv_ln_b': 0.01 * nrm(ks[9], (DEPTH, CONV_CH), f32),
        'w_conv_out': nrm(ks[10], (DEPTH, CONV_CH, D_MODEL), f32) * (CONV_CH ** -0.5) * beta,
        'gla_norm_g': 1.0 + 0.01 * nrm(ks[11], (DEPTH, GLA_DV_TOTAL), f32),
        'w_gla_out': nrm(ks[12], (DEPTH, GLA_DV_TOTAL, D_MODEL), f32) * (GLA_DV_TOTAL ** -0.5) * beta,
        'w_out': nrm(ks[13], (DEPTH, D_MODEL, D_MODEL), f32) * (D_MODEL ** -0.5) * beta,
        'ln_g': 1.0 + 0.01 * nrm(ks[14], (DEPTH, 3, D_MODEL), f32),
        'ln_b': 0.01 * nrm(ks[15], (DEPTH, 3, D_MODEL), f32),
        'w_router': nrm(ks[16], (D_MODEL, N_EXPERTS), f32) * (D_MODEL ** -0.5),
        'b_router': 0.01 * nrm(ks[17], (N_EXPERTS,), f32),
        'w_exp_gate': nrm(ks[18], (DEPTH, N_EXPERTS, D_MODEL, D_FF_EXPERT), f32) * (D_MODEL ** -0.5),
        'w_exp_up': nrm(ks[19], (DEPTH, N_EXPERTS, D_MODEL, D_FF_EXPERT), f32) * (D_MODEL ** -0.5) * beta,
        'w_exp_down': nrm(ks[20], (DEPTH, N_EXPERTS, D_FF_EXPERT, D_MODEL), f32) * (D_FF_EXPERT ** -0.5) * beta,
        'w_ple': nrm(ks[21], (DEPTH, PLE_DIM, D_MODEL), f32) * (PLE_DIM ** -0.5) * beta,
        'w_ple_gate': nrm(ks[22], (DEPTH, D_MODEL, D_MODEL), f32) * (D_MODEL ** -0.5),
    }


def reference(x, p, w_in, b_branch_gate, w_gla_gate_up, b_gla_gate, conv_w, conv_b,
              conv_ln_g, conv_ln_b, w_conv_out, gla_norm_g, w_gla_out, w_out, ln_g, ln_b,
              w_router, b_router, w_exp_gate, w_exp_up, w_exp_down, w_ple, w_ple_gate):
    for i in range(DEPTH):
        m = token_mixer(x, w_in[i], b_branch_gate[i], w_gla_gate_up[i], b_gla_gate[i],
                        conv_w[i], conv_b[i], conv_ln_g[i], conv_ln_b[i], w_conv_out[i],
                        gla_norm_g[i], w_gla_out[i], w_out[i])
        x = layer_norm(DEEPNORM_ALPHA * x + m, ln_g[i, 0], ln_b[i, 0])
        f = group_moe(x, w_router, b_router, w_exp_gate[i], w_exp_up[i], w_exp_down[i])
        x = layer_norm(DEEPNORM_ALPHA * x + f, ln_g[i, 1], ln_b[i, 1])
        e = jax.nn.sigmoid(x @ w_ple_gate[i]) * (p[i].astype(x.dtype) @ w_ple[i])
        x = layer_norm(DEEPNORM_ALPHA * x + e, ln_g[i, 2], ln_b[i, 2])
    return x
```

```python
import functools

import jax
import jax.numpy as jnp
from jax import lax
from jax.experimental import pallas as pl
from jax.experimental.pallas import tpu as pltpu

F32 = jnp.float32
BF16 = jnp.bfloat16

GLA_HEADS = 4
GLA_GATE_TEMP = 16.0
GLA_CHUNK = 64
N_GROUPS = 4
LN_EPS = 1e-5
LANES = 128
V7X_SCOPED_VMEM_BYTES = 60000 * 1024


def _pick(n, pref, mult=LANES):
    if n <= pref:
        return n
    t = (pref // mult) * mult
    while t >= mult:
        if n % t == 0:
            return t
        t -= mult
    return n


def _vmem_limit(nbytes):
    return int(min(max(nbytes * 5 // 4 + (4 << 20), 16 << 20), V7X_SCOPED_VMEM_BYTES))


def _split3(x):
    hi = x.astype(BF16)
    r1 = x - hi.astype(F32)
    mid = r1.astype(BF16)
    lo = (r1 - mid.astype(F32)).astype(BF16)
    return hi, mid, lo


def _dot(a, b):
    return jnp.dot(a, b, preferred_element_type=F32)


def _dot_f32(a, b):
    ah, am, _ = _split3(a)
    bh, bm, _ = _split3(b)
    return _dot(ah, bh) + (_dot(ah, bm) + _dot(am, bh))


def _fused_matmul_kernel(*refs, n_acts, dot_act, cast, n_extras, n_outs, epilogue):
    nd = len(dot_act)
    act_refs = refs[:n_acts]
    w_refs = refs[n_acts:n_acts + nd]
    ex_refs = refs[n_acts + nd:n_acts + nd + n_extras]
    out_refs = refs[n_acts + nd + n_extras:n_acts + nd + n_extras + n_outs]
    scratch = list(refs[n_acts + nd + n_extras + n_outs:])

    w_bf16 = []
    to_cast = []
    for w_ref, c in zip(w_refs, cast):
        if c:
            s = scratch.pop(0)
            to_cast.append((w_ref, s))
            w_bf16.append(s)
        else:
            w_bf16.append(w_ref)
    if to_cast:
        @pl.when(pl.program_id(1) == 0)
        def _():
            for w_ref, s in to_cast:
                s[...] = w_ref[...].astype(BF16)

    dots = [_dot(act_refs[ai][...], w[...]) for ai, w in zip(dot_act, w_bf16)]
    outs = epilogue(dots, [e[...] for e in ex_refs])
    for o_ref, o in zip(out_refs, outs):
        o_ref[...] = o.astype(o_ref.dtype)


def _fused_matmul(acts, dots, extras, epilogue, out_dtypes, *, n_cols, tm, tn, name):
    m = acts[0].shape[0]
    assert m % tm == 0 and n_cols % tn == 0
    grid = (n_cols // tn, m // tm)
    in_specs, args, scratch, cast = [], [], [], []
    nbytes = 0
    for a in acts:
        k = a.shape[1]
        in_specs.append(pl.BlockSpec((tm, k), lambda j, i: (i, 0)))
        args.append(a)
        nbytes += 2 * tm * k * a.dtype.itemsize
    for (_, w, layer, off) in dots:
        k = w.shape[1]
        assert off % tn == 0
        in_specs.append(pl.BlockSpec((None, k, tn), lambda j, i, layer=layer, cb=off // tn: (layer, 0, cb + j)))
        args.append(w)
        nbytes += 2 * k * tn * w.dtype.itemsize
        cast.append(w.dtype != BF16)
        if cast[-1]:
            scratch.append(pltpu.VMEM((k, tn), BF16))
            nbytes += k * tn * 2
    for (kind, arr, layer, off) in extras:
        assert off % tn == 0
        if kind == 'tile':
            in_specs.append(pl.BlockSpec((tm, tn), lambda j, i, cb=off // tn: (i, cb + j)))
            nbytes += 2 * tm * tn * arr.dtype.itemsize
        else:
            in_specs.append(pl.BlockSpec((None, 1, tn), lambda j, i, layer=layer, cb=off // tn: (layer, 0, cb + j)))
        args.append(arr)
    out_specs = [pl.BlockSpec((tm, tn), lambda j, i: (i, j)) for _ in out_dtypes]
    out_shape = [jax.ShapeDtypeStruct((m, n_cols), dt) for dt in out_dtypes]
    nbytes += sum(2 * tm * tn * jnp.dtype(dt).itemsize for dt in out_dtypes)
    nbytes += (len(dots) + 3) * tm * tn * 4
    kernel = functools.partial(
        _fused_matmul_kernel, n_acts=len(acts), dot_act=tuple(d[0] for d in dots), cast=tuple(cast),
        n_extras=len(extras), n_outs=len(out_dtypes), epilogue=epilogue)
    return pl.pallas_call(
        kernel, grid=grid, in_specs=in_specs, out_specs=out_specs, out_shape=out_shape,
        scratch_shapes=scratch, name=name,
        compiler_params=pltpu.CompilerParams(
            dimension_semantics=("arbitrary", "arbitrary"), vmem_limit_bytes=_vmem_limit(nbytes)),
    )(*args)


def _conv_kernel(u_ref, prev_ref, w_ref, b_ref, g_ref, beta_ref, o_ref, win, ybuf, *, kw, halo, rows):
    i = pl.program_id(0)
    j = pl.program_id(1)
    tt, cc = u_ref.shape
    win[pl.ds(0, halo), :] = jnp.where(i > 0, prev_ref[...], 0.0)
    win[pl.ds(halo, tt), :] = u_ref[...]
    lead = halo - (kw - 1)
    for r0 in range(0, tt, rows):
        acc = jnp.broadcast_to(b_ref[...], (rows, cc))
        for t in range(kw):
            acc = acc + win[pl.ds(r0 + lead + t, rows), :] * w_ref[pl.ds(t, 1), :]
        ybuf[j, pl.ds(r0, rows), :] = acc

    @pl.when(j == pl.num_programs(1) - 1)
    def _():
        nc = ybuf.shape[0]
        c = nc * cc
        s = jnp.zeros((tt, 1), F32)
        for k in range(nc):
            s = s + jnp.sum(ybuf[k], axis=-1, keepdims=True)
        mu = s * (1.0 / c)
        v = jnp.zeros((tt, 1), F32)
        for k in range(nc):
            d = ybuf[k] - mu
            v = v + jnp.sum(d * d, axis=-1, keepdims=True)
        rstd = lax.rsqrt(v * (1.0 / c) + LN_EPS)
        for k in range(nc):
            cs = slice(k * cc, (k + 1) * cc)
            y = (ybuf[k] - mu) * rstd * g_ref[:, cs] + beta_ref[:, cs]
            o_ref[:, cs] = (y * jax.nn.sigmoid(y)).astype(o_ref.dtype)


def _conv_ln_silu(u, conv_w, conv_b, ln_g, ln_b, layer):
    m, c = u.shape
    kw = conv_w.shape[1]
    halo = -(-(kw - 1) // 8) * 8
    tt = _pick(m, 512, halo)
    cc = _pick(c, 256)
    rows = 64 if tt % 64 == 0 else tt
    kernel = functools.partial(_conv_kernel, kw=kw, halo=halo, rows=rows)
    return pl.pallas_call(
        kernel, grid=(m // tt, c // cc),
        in_specs=[
            pl.BlockSpec((tt, cc), lambda i, j: (i, j)),
            pl.BlockSpec((halo, cc), lambda i, j: (jnp.maximum(i * (tt // halo) - 1, 0), j)),
            pl.BlockSpec((None, kw, cc), lambda i, j: (layer, 0, j)),
            pl.BlockSpec((None, 1, cc), lambda i, j: (layer, 0, j)),
            pl.BlockSpec((None, 1, c), lambda i, j: (layer, 0, 0)),
            pl.BlockSpec((None, 1, c), lambda i, j: (layer, 0, 0)),
        ],
        out_specs=pl.BlockSpec((tt, c), lambda i, j: (i, 0)),
        out_shape=jax.ShapeDtypeStruct((m, c), BF16),
        scratch_shapes=[pltpu.VMEM((tt + halo, cc), F32), pltpu.VMEM((c // cc, tt, cc), F32)],
        name="conv_ln_silu",
        compiler_params=pltpu.CompilerParams(
            dimension_semantics=("arbitrary", "arbitrary"),
            vmem_limit_bytes=_vmem_limit(3 * tt * c * 4 + 4 * tt * cc * 4)),
    )(u, u, conv_w, conv_b, ln_g, ln_b)


def _gla_kernel(q_ref, k_ref, v_ref, r_ref, a_ref, wup_ref, ba_ref, g_ref, o_ref,
                state, qe_s, qi_s, ki_s, kd_s, dec_s, obuf, *, rank, chunk):
    tb, dk = q_ref.shape
    dv = v_ref.shape[1]
    hk, hv = dk // GLA_HEADS, dv // GLA_HEADS
    nchunks = tb // chunk

    @pl.when(pl.program_id(0) == 0)
    def _():
        state[...] = jnp.zeros_like(state)

    lane = lax.broadcasted_iota(jnp.int32, a_ref.shape, 1)
    a_low = jnp.where(lane < rank, a_ref[...], 0.0)
    z = _dot_f32(a_low, wup_ref[...]) + ba_ref[...]
    la = (jnp.minimum(z, 0.0) - jnp.log1p(jnp.exp(-jnp.abs(z)))) * (1.0 / GLA_GATE_TEMP)
    la3 = _split3(la)

    rr = lax.broadcasted_iota(jnp.int32, (tb, tb), 0)
    cc = lax.broadcasted_iota(jnp.int32, (tb, tb), 1)
    sh = chunk.bit_length() - 1
    assert chunk == 1 << sh
    same = (rr >> sh) == (cc >> sh)
    tril = jnp.where(same & (cc <= rr), 1.0, 0.0).astype(BF16)
    ones_blk = jnp.where(same, 1.0, 0.0).astype(BF16)
    b = _dot(tril, la3[0]) + (_dot(tril, la3[1]) + _dot(tril, la3[2]))
    bl = _dot(ones_blk, la3[0]) + (_dot(ones_blk, la3[1]) + _dot(ones_blk, la3[2]))

    sr = lax.broadcasted_iota(jnp.int32, (tb, nchunks * LANES), 0)
    sc = lax.broadcasted_iota(jnp.int32, (tb, nchunks * LANES), 1)
    sel = jnp.where((sr >> sh) == (sc >> (LANES.bit_length() - 1)), 1.0, 0.0).astype(BF16)
    tn_dims = (((0,), (0,)), ((), ()))
    tot_t = (lax.dot_general(la3[0], sel, tn_dims, preferred_element_type=F32)
             + (lax.dot_general(la3[1], sel, tn_dims, preferred_element_type=F32)
                + lax.dot_general(la3[2], sel, tn_dims, preferred_element_type=F32)))
    dec_s[...] = jnp.exp(tot_t)

    q = q_ref[...].astype(F32) * (float(hk) ** -0.5)
    k = k_ref[...].astype(F32)
    half = 0.5 * bl
    qe_s[...] = (q * jnp.exp(b)).astype(BF16)
    qi_s[...] = (q * jnp.exp(b - half)).astype(BF16)
    ki_s[...] = (k * jnp.exp(half - b)).astype(BF16)
    kd_s[...] = (k * jnp.exp(bl - b)).astype(BF16)

    ri = lax.broadcasted_iota(jnp.int32, (chunk, chunk), 0)
    ci = lax.broadcasted_iota(jnp.int32, (chunk, chunk), 1)
    causal = ci <= ri
    nt_dims = (((1,), (1,)), ((), ()))
    for c in range(nchunks):
        rows = pl.ds(c * chunk, chunk)
        for h in range(GLA_HEADS):
            ks = pl.ds(h * hk, hk)
            vs = pl.ds(h * hv, hv)
            vch = v_ref[rows, vs].astype(BF16)
            scores = lax.dot_general(qi_s[rows, ks], ki_s[rows, ks], nt_dims, preferred_element_type=F32)
            scores = jnp.where(causal, scores, 0.0).astype(BF16)
            s_old = state[h]
            o = _dot(qe_s[rows, ks], s_old.astype(BF16)) + _dot(scores, vch)
            obuf[rows, vs] = o
            dcol = dec_s[ks, pl.ds(c * LANES, LANES)]
            decay = jnp.concatenate([dcol] * (hv // LANES), axis=1) if hv >= LANES else dcol[:, :hv]
            state[h] = s_old * decay + lax.dot_general(kd_s[rows, ks], vch, tn_dims, preferred_element_type=F32)

    for h in range(GLA_HEADS):
        vs = pl.ds(h * hv, hv)
        o = obuf[:, vs]
        ms = jnp.mean(o * o, axis=-1, keepdims=True)
        on = o * lax.rsqrt(ms + LN_EPS) * g_ref[:, vs]
        r = r_ref[:, vs].astype(F32)
        o_ref[:, vs] = (on * (r * jax.nn.sigmoid(r))).astype(o_ref.dtype)


def _gla(h2, a_low, wup_pad, b_a, norm_g, layer, *, dk, dv, rank):
    m = h2.shape[0]
    tb = _pick(m, 256, GLA_CHUNK)
    assert tb % GLA_CHUNK == 0 and dv == 2 * dk
    nchunks = tb // GLA_CHUNK
    hk, hv = dk // GLA_HEADS, dv // GLA_HEADS
    kernel = functools.partial(_gla_kernel, rank=rank, chunk=GLA_CHUNK)
    return pl.pallas_call(
        kernel, grid=(m // tb,),
        in_specs=[
            pl.BlockSpec((tb, dk), lambda i: (i, 0)),
            pl.BlockSpec((tb, dk), lambda i: (i, 1)),
            pl.BlockSpec((tb, dv), lambda i: (i, 1)),
            pl.BlockSpec((tb, dv), lambda i: (i, 2)),
            pl.BlockSpec((tb, LANES), lambda i: (i, 0)),
            pl.BlockSpec((None, LANES, dk), lambda i: (layer, 0, 0)),
            pl.BlockSpec((None, 1, dk), lambda i: (layer, 0, 0)),
            pl.BlockSpec((None, 1, dv), lambda i: (layer, 0, 0)),
        ],
        out_specs=pl.BlockSpec((tb, dv), lambda i: (i, 0)),
        out_shape=jax.ShapeDtypeStruct((m, dv), BF16),
        scratch_shapes=[
            pltpu.VMEM((GLA_HEADS, hk, hv), F32),
            pltpu.VMEM((tb, dk), BF16), pltpu.VMEM((tb, dk), BF16),
            pltpu.VMEM((tb, dk), BF16), pltpu.VMEM((tb, dk), BF16),
            pltpu.VMEM((dk, nchunks * LANES), F32),
            pltpu.VMEM((tb, dv), F32),
        ],
        name="gla",
        compiler_params=pltpu.CompilerParams(
            dimension_semantics=("arbitrary",), vmem_limit_bytes=_vmem_limit(48 << 20)),
    )(h2, h2, h2, h2, a_low, wup_pad, b_a, norm_g)


def _ln_kernel(z_ref, g_ref, b_ref, *rest, with_router):
    if with_router:
        wr_ref, br_ref, x_ref, xb_ref, lg_ref = rest
    else:
        x_ref, xb_ref = rest
    z = z_ref[...]
    mu = jnp.mean(z, axis=-1, keepdims=True)
    d = z - mu
    var = jnp.mean(d * d, axis=-1, keepdims=True)
    x = d * lax.rsqrt(var + LN_EPS) * g_ref[...] + b_ref[...]
    x_ref[...] = x
    xb_ref[...] = x.astype(BF16)
    if with_router:
        lg_ref[...] = _dot_f32(x, wr_ref[...]) + br_ref[...]


def _layer_norm(z, ln_g, ln_b, idx, router=None):
    m, d = z.shape
    tr = _pick(m, 512, 8)
    in_specs = [
        pl.BlockSpec((tr, d), lambda i: (i, 0)),
        pl.BlockSpec((None, 1, d), lambda i: (idx, 0, 0)),
        pl.BlockSpec((None, 1, d), lambda i: (idx, 0, 0)),
    ]
    args = [z, ln_g, ln_b]
    out_specs = [pl.BlockSpec((tr, d), lambda i: (i, 0)), pl.BlockSpec((tr, d), lambda i: (i, 0))]
    out_shape = [jax.ShapeDtypeStruct((m, d), F32), jax.ShapeDtypeStruct((m, d), BF16)]
    if router is not None:
        in_specs += [pl.BlockSpec((d, LANES), lambda i: (0, 0)), pl.BlockSpec((1, LANES), lambda i: (0, 0))]
        args += list(router)
        out_specs.append(pl.BlockSpec((tr, LANES), lambda i: (i, 0)))
        out_shape.append(jax.ShapeDtypeStruct((m, LANES), F32))
    return pl.pallas_call(
        functools.partial(_ln_kernel, with_router=router is not None),
        grid=(m // tr,), in_specs=in_specs, out_specs=out_specs, out_shape=out_shape,
        name="layer_norm_router" if router is not None else "layer_norm",
        compiler_params=pltpu.CompilerParams(
            dimension_semantics=("arbitrary",), vmem_limit_bytes=_vmem_limit(tr * d * 36)),
    )(*args)


def _route_kernel(lg_ref, idx_ref, w_ref, *, n_experts):
    per = n_experts // N_GROUPS
    logit = [lg_ref[e] for e in range(n_experts)]
    mx = functools.reduce(jnp.maximum, logit)
    ex = [jnp.exp(l - mx) for l in logit]
    tot = functools.reduce(lambda a, c: a + c, ex)
    p = [e / tot for e in ex]

    def top2_sum(vals):
        a, bb, c, d = vals
        hi1, lo1 = jnp.maximum(a, bb), jnp.minimum(a, bb)
        hi2, lo2 = jnp.maximum(c, d), jnp.minimum(c, d)
        first = jnp.maximum(hi1, hi2)
        second = jnp.maximum(jnp.minimum(hi1, hi2), jnp.maximum(lo1, lo2))
        return first + second

    assert per == 4
    gs = [top2_sum(p[g * per:(g + 1) * per]) for g in range(N_GROUPS)]
    best = jnp.zeros_like(gs[0], dtype=jnp.int32)
    best_v = gs[0]
    for g in range(1, N_GROUPS):
        upd = gs[g] > best_v
        best = jnp.where(upd, g, best)
        best_v = jnp.where(upd, gs[g], best_v)
    vals = []
    for k in range(per):
        v = p[k]
        for g in range(1, N_GROUPS):
            v = jnp.where(best == g, p[g * per + k], v)
        vals.append(v)
    i1 = jnp.zeros_like(best)
    v1 = vals[0]
    for k in range(1, per):
        upd = vals[k] > v1
        i1 = jnp.where(upd, k, i1)
        v1 = jnp.where(upd, vals[k], v1)
    i2 = jnp.full_like(best, -1)
    v2 = jnp.full_like(v1, -1.0)
    for k in range(per):
        upd = (i1 != k) & (vals[k] > v2)
        i2 = jnp.where(upd, k, i2)
        v2 = jnp.where(upd, vals[k], v2)
    den = v1 + v2
    idx_ref[0] = best * per + i1
    idx_ref[1] = best * per + i2
    w_ref[0] = v1 / den
    w_ref[1] = v2 / den


def _route(logits_t):
    n_experts, rows, lanes = logits_t.shape
    return pl.pallas_call(
        functools.partial(_route_kernel, n_experts=n_experts),
        out_shape=[jax.ShapeDtypeStruct((2, rows, lanes), jnp.int32),
                   jax.ShapeDtypeStruct((2, rows, lanes), F32)],
        name="route",
    )(logits_t)


def _gather_kernel(tok_ref, nt_ref, x_hbm, o_ref, buf, sem):
    t = pl.program_id(0)
    tm = buf.shape[0]

    def row_copy(r, tok):
        return pltpu.make_async_copy(x_hbm.at[pl.ds(tok, 1), :], buf.at[pl.ds(r, 1), :], sem)

    @pl.when(t < nt_ref[0])
    def _():
        def issue(r, carry):
            row_copy(r, tok_ref[t * tm + r]).start()
            return carry
        lax.fori_loop(0, tm, issue, 0)

        def drain(r, carry):
            row_copy(r, 0).wait()
            return carry
        lax.fori_loop(0, tm, drain, 0)
        o_ref[...] = buf[...].astype(o_ref.dtype)

    @pl.when(t >= nt_ref[0])
    def _():
        o_ref[...] = jnp.zeros_like(o_ref)


def _gather_rows(x, row_token, n_tiles, *, tm):
    m, d = x.shape
    npad = row_token.shape[0]
    return pl.pallas_call(
        _gather_kernel,
        grid_spec=pltpu.PrefetchScalarGridSpec(
            num_scalar_prefetch=2, grid=(npad // tm,),
            in_specs=[pl.BlockSpec(memory_space=pl.ANY)],
            out_specs=pl.BlockSpec((tm, d), lambda t, tok, nt: (t, 0)),
            scratch_shapes=[pltpu.VMEM((tm, d), F32), pltpu.SemaphoreType.DMA(())]),
        out_shape=jax.ShapeDtypeStruct((npad, d), BF16),
        name="expert_gather",
        compiler_params=pltpu.CompilerParams(
            dimension_semantics=("arbitrary",), vmem_limit_bytes=_vmem_limit(tm * d * 12)),
    )(row_token, n_tiles, x)


def _new_expert(te_ref, t):
    return (t == 0) | (te_ref[t] != te_ref[jnp.maximum(t - 1, 0)])


def _expert_up_kernel(te_ref, nt_ref, x_ref, wg_ref, wu_ref, o_ref, wg_s, wu_s):
    t = pl.program_id(1)
    live = t < nt_ref[0]

    @pl.when(live & _new_expert(te_ref, t))
    def _():
        wg_s[...] = wg_ref[...].astype(BF16)
        wu_s[...] = wu_ref[...].astype(BF16)

    @pl.when(live)
    def _():
        x = x_ref[...]
        g = _dot(x, wg_s[...])
        u = _dot(x, wu_s[...])
        o_ref[...] = (g * jax.nn.sigmoid(g) * u).astype(o_ref.dtype)

    @pl.when(jnp.logical_not(live))
    def _():
        o_ref[...] = jnp.zeros_like(o_ref)


def _expert_up(xg, w_gate, w_up, layer, tile_e, n_tiles, *, tm):
    npad, d = xg.shape
    f = w_gate.shape[-1]
    tf = _pick(f, 512)
    w_spec = pl.BlockSpec((None, None, d, tf), lambda j, t, te, nt: (layer, te[t], 0, j))
    return pl.pallas_call(
        _expert_up_kernel,
        grid_spec=pltpu.PrefetchScalarGridSpec(
            num_scalar_prefetch=2, grid=(f // tf, npad // tm),
            in_specs=[pl.BlockSpec((tm, d), lambda j, t, te, nt: (t, 0)), w_spec, w_spec],
            out_specs=pl.BlockSpec((tm, tf), lambda j, t, te, nt: (t, j)),
            scratch_shapes=[pltpu.VMEM((d, tf), BF16), pltpu.VMEM((d, tf), BF16)]),
        out_shape=jax.ShapeDtypeStruct((npad, f), BF16),
        name="expert_up",
        compiler_params=pltpu.CompilerParams(
            dimension_semantics=("arbitrary", "arbitrary"),
            vmem_limit_bytes=_vmem_limit(4 * d * tf * 4 + 2 * d * tf * 2 + 2 * tm * d * 2 + 8 * tm * tf * 4)),
    )(tile_e, n_tiles, xg, w_gate, w_up)


def _expert_down_kernel(te_ref, nt_ref, h_ref, wd_ref, rw_ref, o_ref, wd_s):
    t = pl.program_id(1)
    live = t < nt_ref[0]

    @pl.when(live & _new_expert(te_ref, t))
    def _():
        wd_s[...] = wd_ref[...].astype(BF16)

    @pl.when(live)
    def _():
        o_ref[...] = rw_ref[...] * _dot(h_ref[...], wd_s[...])

    @pl.when(jnp.logical_not(live))
    def _():
        o_ref[...] = jnp.zeros_like(o_ref)


def _expert_down(hg, w_down, layer, tile_e, n_tiles, row_w, *, tm):
    npad, f = hg.shape
    d = w_down.shape[-1]
    tn = _pick(d, 1024)
    return pl.pallas_call(
        _expert_down_kernel,
        grid_spec=pltpu.PrefetchScalarGridSpec(
            num_scalar_prefetch=2, grid=(d // tn, npad // tm),
            in_specs=[pl.BlockSpec((tm, f), lambda j, t, te, nt: (t, 0)),
                      pl.BlockSpec((None, None, f, tn), lambda j, t, te, nt: (layer, te[t], 0, j)),
                      pl.BlockSpec((tm, 1), lambda j, t, te, nt: (t, 0))],
            out_specs=pl.BlockSpec((tm, tn), lambda j, t, te, nt: (t, j)),
            scratch_shapes=[pltpu.VMEM((f, tn), BF16)]),
        out_shape=jax.ShapeDtypeStruct((npad, d), F32),
        name="expert_down",
        compiler_params=pltpu.CompilerParams(
            dimension_semantics=("arbitrary", "arbitrary"),
            vmem_limit_bytes=_vmem_limit(2 * f * tn * 4 + f * tn * 2 + 2 * tm * f * 2 + 6 * tm * tn * 4)),
    )(tile_e, n_tiles, hg, w_down, row_w)


def _combine_ln_kernel(pos_ref, y_hbm, x_ref, g_ref, b_ref, o_ref, ob_ref, buf, sem, *, alpha, m):
    i = pl.program_id(0)
    tc = x_ref.shape[0]

    def row_copy(s, r, p):
        return pltpu.make_async_copy(y_hbm.at[pl.ds(p, 1), :], buf.at[s, pl.ds(r, 1), :], sem)

    def issue(r, carry):
        row_copy(0, r, pos_ref[i * tc + r]).start()
        row_copy(1, r, pos_ref[m + i * tc + r]).start()
        return carry
    lax.fori_loop(0, tc, issue, 0)

    def drain(r, carry):
        row_copy(0, r, 0).wait()
        row_copy(1, r, 0).wait()
        return carry
    lax.fori_loop(0, tc, drain, 0)

    z = alpha * x_ref[...] + (buf[0] + buf[1])
    mu = jnp.mean(z, axis=-1, keepdims=True)
    dz = z - mu
    var = jnp.mean(dz * dz, axis=-1, keepdims=True)
    x = dz * lax.rsqrt(var + LN_EPS) * g_ref[...] + b_ref[...]
    o_ref[...] = x
    ob_ref[...] = x.astype(BF16)


def _combine_ln(yg, pos, x, ln_g, ln_b, idx, *, alpha):
    m, d = x.shape
    tc = _pick(m, 256, 8)
    row = lambda i, pos: (i, 0)
    par = lambda i, pos: (idx, 0, 0)
    return pl.pallas_call(
        functools.partial(_combine_ln_kernel, alpha=alpha, m=m),
        grid_spec=pltpu.PrefetchScalarGridSpec(
            num_scalar_prefetch=1, grid=(m // tc,),
            in_specs=[pl.BlockSpec(memory_space=pl.ANY),
                      pl.BlockSpec((tc, d), row),
                      pl.BlockSpec((None, 1, d), par), pl.BlockSpec((None, 1, d), par)],
            out_specs=[pl.BlockSpec((tc, d), row), pl.BlockSpec((tc, d), row)],
            scratch_shapes=[pltpu.VMEM((2, tc, d), F32), pltpu.SemaphoreType.DMA(())]),
        out_shape=[jax.ShapeDtypeStruct((m, d), F32), jax.ShapeDtypeStruct((m, d), BF16)],
        name="combine_ln",
        compiler_params=pltpu.CompilerParams(
            dimension_semantics=("arbitrary",), vmem_limit_bytes=_vmem_limit(tc * d * 44)),
    )(pos, yg, x, ln_g, ln_b)


def _expert_schedule(eidx, ewts, n_experts, tm):
    m = eidx.shape[1]
    e_flat = eidx.reshape(-1)
    onehot = (e_flat[:, None] == jnp.arange(n_experts, dtype=jnp.int32)[None, :]).astype(jnp.int32)
    csum = jnp.cumsum(onehot, axis=0)
    rank = jnp.take_along_axis(csum, e_flat[:, None], axis=1)[:, 0] - 1
    counts = csum[-1]
    tiles = (counts + tm - 1) // tm
    tile_end = jnp.cumsum(tiles)
    start = (tile_end - tiles) * tm
    pos = (start[e_flat] + rank).astype(jnp.int32)
    npad = 2 * m + n_experts * tm
    tok = jnp.concatenate([jnp.arange(m, dtype=jnp.int32)] * 2)
    row_token = jnp.zeros((npad,), jnp.int32).at[pos].set(tok)
    row_w = jnp.zeros((npad,), F32).at[pos].set(ewts.reshape(-1))
    n_tiles = tile_end[-1].astype(jnp.int32)
    tile_ids = jnp.arange(npad // tm, dtype=jnp.int32)
    tile_e = jnp.searchsorted(tile_end, jnp.minimum(tile_ids, n_tiles - 1), side='right').astype(jnp.int32)
    return row_token, row_w.reshape(npad, 1), pos, tile_e, n_tiles.reshape(1)


def kernel(x, p, w_in, b_branch_gate, w_gla_gate_up, b_gla_gate, conv_w, conv_b, conv_ln_g, conv_ln_b, w_conv_out, gla_norm_g, w_gla_out, w_out, ln_g, ln_b, w_router, b_router, w_exp_gate, w_exp_up, w_exp_down, w_ple, w_ple_gate):
    bsz, seq, d = x.shape
    assert bsz == 1
    depth = w_in.shape[0]
    m = bsz * seq
    conv_ch = conv_w.shape[-1]
    rank, dk = w_gla_gate_up.shape[1], w_gla_gate_up.shape[2]
    dv = gla_norm_g.shape[-1]
    n_experts = w_router.shape[1]
    off_b = conv_ch
    off_q = 2 * conv_ch
    off_alpha = off_q + 2 * dk + 2 * dv
    off_gate = off_alpha + rank
    alpha = (2.0 * depth) ** 0.25
    tm_e = 256 if m >= 2048 else 64

    xf = x.reshape(m, d)
    xb = xf.astype(BF16)
    pb = p.reshape(depth, m, p.shape[-1]).astype(BF16)
    w_gate_b = w_in[:, :, off_gate:].astype(BF16)
    b_gate = b_branch_gate.reshape(depth, 1, 2 * d)
    wup_pad = jnp.zeros((depth, LANES, dk), F32).at[:, :rank, :].set(w_gla_gate_up)
    b_a = b_gla_gate.reshape(depth, 1, dk)
    conv_b3 = conv_b.reshape(depth, 1, conv_ch)
    cln_g = conv_ln_g.reshape(depth, 1, conv_ch)
    cln_b = conv_ln_b.reshape(depth, 1, conv_ch)
    gnorm = gla_norm_g.reshape(depth, 1, dv)
    ln_g3 = ln_g.reshape(depth * 3, 1, d)
    ln_b3 = ln_b.reshape(depth * 3, 1, d)
    wr_pad = jnp.zeros((d, LANES), F32).at[:, :n_experts].set(w_router)
    br_pad = jnp.zeros((1, LANES), F32).at[0, :n_experts].set(b_router)

    tm = _pick(m, 1024, 8)
    for i in range(depth):
        (u,) = _fused_matmul(
            [xb], [(0, w_in, i, 0), (0, w_in, i, off_b)], [],
            lambda ds, ex: [ds[0] * jax.nn.sigmoid(ds[1])], [F32],
            n_cols=conv_ch, tm=tm, tn=_pick(conv_ch, 256), name="in_proj_glu")
        (h2,) = _fused_matmul(
            [xb], [(0, w_in, i, off_q)], [], lambda ds, ex: [ds[0]], [F32],
            n_cols=2 * dk + 2 * dv, tm=tm, tn=_pick(dk, 512), name="in_proj_qkvr")
        (a_low,) = _fused_matmul(
            [xb], [(0, w_in, i, off_alpha)], [], lambda ds, ex: [ds[0]], [F32],
            n_cols=LANES, tm=tm, tn=LANES, name="in_proj_alpha")
        (gates,) = _fused_matmul(
            [xb], [(0, w_gate_b, i, 0)], [('row', b_gate, i, 0)],
            lambda ds, ex: [jax.nn.sigmoid(ds[0] + ex[0])], [BF16],
            n_cols=2 * d, tm=tm, tn=_pick(d, 512), name="in_proj_gates")

        uc = _conv_ln_silu(u, conv_w, conv_b3, cln_g, cln_b, i)
        og = _gla(h2, a_low, wup_pad, b_a, gnorm, i, dk=dk, dv=dv, rank=rank)

        tn_o = _pick(d, 512)
        (merged,) = _fused_matmul(
            [uc, og], [(0, w_conv_out, i, 0), (1, w_gla_out, i, 0)],
            [('tile', gates, None, 0), ('tile', gates, None, d)],
            lambda ds, ex: [ex[0].astype(F32) * ds[0] + ex[1].astype(F32) * ds[1]], [BF16],
            n_cols=d, tm=tm, tn=tn_o, name="branch_out_merge")
        (z1,) = _fused_matmul(
            [merged], [(0, w_out, i, 0)], [('tile', xf, None, 0)],
            lambda ds, ex: [alpha * ex[0] + ds[0]], [F32],
            n_cols=d, tm=tm, tn=tn_o, name="mixer_out")
        x1, x1b, logits = _layer_norm(z1, ln_g3, ln_b3, 3 * i, router=(wr_pad, br_pad))

        logits_t = logits[:, :n_experts].T.reshape(n_experts, m // LANES, LANES)
        eidx, ewts = _route(logits_t)
        row_token, row_w, pos, tile_e, n_tiles = _expert_schedule(
            eidx.reshape(2, m), ewts.reshape(2, m), n_experts, tm_e)
        xg = _gather_rows(x1, row_token, n_tiles, tm=tm_e)
        hg = _expert_up(xg, w_exp_gate, w_exp_up, i, tile_e, n_tiles, tm=tm_e)
        yg = _expert_down(hg, w_exp_down, i, tile_e, n_tiles, row_w, tm=tm_e)
        x2, x2b = _combine_ln(yg, pos, x1, ln_g3, ln_b3, 3 * i + 1, alpha=alpha)

        (z3,) = _fused_matmul(
            [x2b, pb[i]], [(0, w_ple_gate, i, 0), (1, w_ple, i, 0)], [('tile', x2, None, 0)],
            lambda ds, ex: [alpha * ex[0] + jax.nn.sigmoid(ds[0]) * ds[1]], [F32],
            n_cols=d, tm=tm, tn=tn_o, name="ple")
        xf, xb = _layer_norm(z3, ln_g3, ln_b3, 3 * i + 2)

    return xf.reshape(bsz, seq, d)
```

```python
import functools

import jax
import jax.numpy as jnp
from jax import lax
from jax.experimental import pallas as pl
from jax.experimental.pallas import tpu as pltpu

F32 = jnp.float32
BF16 = jnp.bfloat16

GLA_HEADS = 4
GLA_GATE_TEMP = 16.0
GLA_CHUNK = 64
N_GROUPS = 4
LN_EPS = 1e-5
LANES = 128
V7X_SCOPED_VMEM_BYTES = 60000 * 1024


def _pick(n, pref, mult=LANES):
    if n <= pref:
        return n
    t = (pref // mult) * mult
    while t >= mult:
        if n % t == 0:
            return t
        t -= mult
    return n


def _vmem_limit(nbytes):
    return int(min(max(nbytes * 5 // 4 + (4 << 20), 16 << 20), V7X_SCOPED_VMEM_BYTES))


def _split3(x):
    hi = x.astype(BF16)
    r1 = x - hi.astype(F32)
    mid = r1.astype(BF16)
    lo = (r1 - mid.astype(F32)).astype(BF16)
    return hi, mid, lo


def _dot(a, b):
    return jnp.dot(a, b, preferred_element_type=F32)


def _dot_f32(a, b):
    ah, am, _ = _split3(a)
    bh, bm, _ = _split3(b)
    return _dot(ah, bh) + (_dot(ah, bm) + _dot(am, bh))


def _fused_matmul_kernel(*refs, n_acts, dot_act, shifts, n_extras, n_outs, epilogue):
    nd = len(dot_act)
    n_w = nd + sum(1 for s in shifts if s)
    act_refs = refs[:n_acts]
    w_refs = list(refs[n_acts:n_acts + n_w])
    ex_refs = refs[n_acts + n_w:n_acts + n_w + n_extras]
    out_refs = refs[n_acts + n_w + n_extras:n_acts + n_w + n_extras + n_outs]
    w_bf16 = refs[n_acts + n_w + n_extras + n_outs:]

    @pl.when(pl.program_id(1) == 0)
    def _():
        for s, shift in zip(w_bf16, shifts):
            w = w_refs.pop(0)[...].astype(BF16)
            if shift:
                tn = w.shape[1]
                w = jnp.concatenate([w, w_refs.pop(0)[...].astype(BF16)], axis=1)[:, shift:shift + tn]
            s[...] = w

    dots = [_dot(act_refs[ai][...], w[...]) for ai, w in zip(dot_act, w_bf16)]
    outs = epilogue(dots, [e[...] for e in ex_refs])
    for o_ref, o in zip(out_refs, outs):
        o_ref[...] = o.astype(o_ref.dtype)


def _fused_matmul(acts, dots, extras, epilogue, out_dtypes, *, n_cols, tm, tn, name):
    m = acts[0].shape[0]
    assert m % tm == 0 and n_cols % tn == 0
    grid = (n_cols // tn, m // tm)
    in_specs, args, scratch, shifts = [], [], [], []
    nbytes = 0
    for a in acts:
        k = a.shape[1]
        in_specs.append(pl.BlockSpec((tm, k), lambda j, i: (i, 0)))
        args.append(a)
        nbytes += 2 * tm * k * a.dtype.itemsize
    for (_, w, layer, off) in dots:
        k = w.shape[1]
        shift = off % tn
        assert shift < LANES and tn % LANES == 0
        in_specs.append(pl.BlockSpec((None, k, tn), lambda j, i, layer=layer, cb=off // tn: (layer, 0, cb + j)))
        args.append(w)
        nbytes += 2 * k * tn * w.dtype.itemsize
        if shift:
            in_specs.append(pl.BlockSpec(
                (None, k, LANES),
                lambda j, i, layer=layer, cb=off // tn, r=tn // LANES: (layer, 0, (cb + j + 1) * r)))
            args.append(w)
            nbytes += 2 * k * LANES * w.dtype.itemsize + k * (tn + LANES) * 4
        shifts.append(shift)
        scratch.append(pltpu.VMEM((k, tn), BF16))
        nbytes += k * tn * 2
    for (kind, arr, layer, off) in extras:
        assert off % tn == 0
        if kind == 'tile':
            in_specs.append(pl.BlockSpec((tm, tn), lambda j, i, cb=off // tn: (i, cb + j)))
            nbytes += 2 * tm * tn * arr.dtype.itemsize
        else:
            in_specs.append(pl.BlockSpec((None, 1, tn), lambda j, i, layer=layer, cb=off // tn: (layer, 0, cb + j)))
        args.append(arr)
    out_specs = [pl.BlockSpec((tm, tn), lambda j, i: (i, j)) for _ in out_dtypes]
    out_shape = [jax.ShapeDtypeStruct((m, n_cols), dt) for dt in out_dtypes]
    nbytes += sum(2 * tm * tn * jnp.dtype(dt).itemsize for dt in out_dtypes)
    nbytes += (len(dots) + 3) * tm * tn * 4
    kernel = functools.partial(
        _fused_matmul_kernel, n_acts=len(acts), dot_act=tuple(d[0] for d in dots), shifts=tuple(shifts),
        n_extras=len(extras), n_outs=len(out_dtypes), epilogue=epilogue)
    return pl.pallas_call(
        kernel, grid=grid, in_specs=in_specs, out_specs=out_specs, out_shape=out_shape,
        scratch_shapes=scratch, name=name,
        compiler_params=pltpu.CompilerParams(
            dimension_semantics=("arbitrary", "arbitrary"), vmem_limit_bytes=_vmem_limit(nbytes)),
    )(*args)


def _conv_kernel(u_ref, prev_ref, w_ref, b_ref, g_ref, beta_ref, o_ref, win, shifted, ybuf, *, kw, halo, rows):
    i = pl.program_id(0)
    j = pl.program_id(1)
    tt, cc = u_ref.shape
    win[pl.ds(0, halo), :] = jnp.where(i > 0, prev_ref[...], 0.0)
    win[pl.ds(halo, tt), :] = u_ref[...]
    span = shifted.shape[1]
    for s in range(1, 8):
        shifted[s - 1] = win[pl.ds(s, span), :]
    lead = halo - (kw - 1)
    for r0 in range(0, tt, rows):
        acc = jnp.broadcast_to(b_ref[...], (rows, cc))
        for t in range(kw):
            s = (lead + t) % 8
            start = r0 + lead + t - s
            src = win[pl.ds(start, rows), :] if s == 0 else shifted[s - 1, pl.ds(start, rows), :]
            acc = acc + src * w_ref[pl.ds(t, 1), :]
        ybuf[j, pl.ds(r0, rows), :] = acc

    @pl.when(j == pl.num_programs(1) - 1)
    def _():
        nc = ybuf.shape[0]
        c = nc * cc
        s = jnp.zeros((tt, 1), F32)
        for k in range(nc):
            s = s + jnp.sum(ybuf[k], axis=-1, keepdims=True)
        mu = s * (1.0 / c)
        v = jnp.zeros((tt, 1), F32)
        for k in range(nc):
            d = ybuf[k] - mu
            v = v + jnp.sum(d * d, axis=-1, keepdims=True)
        rstd = lax.rsqrt(v * (1.0 / c) + LN_EPS)
        for k in range(nc):
            cs = slice(k * cc, (k + 1) * cc)
            y = (ybuf[k] - mu) * rstd * g_ref[:, cs] + beta_ref[:, cs]
            o_ref[:, cs] = (y * jax.nn.sigmoid(y)).astype(o_ref.dtype)


def _conv_ln_silu(u, conv_w, conv_b, ln_g, ln_b, layer):
    m, c = u.shape
    kw = conv_w.shape[1]
    halo = -(-(kw - 1) // 8) * 8
    tt = _pick(m, 512, halo)
    cc = _pick(c, 256)
    rows = 64 if tt % 64 == 0 else tt
    kernel = functools.partial(_conv_kernel, kw=kw, halo=halo, rows=rows)
    return pl.pallas_call(
        kernel, grid=(m // tt, c // cc),
        in_specs=[
            pl.BlockSpec((tt, cc), lambda i, j: (i, j)),
            pl.BlockSpec((halo, cc), lambda i, j: (jnp.maximum(i * (tt // halo) - 1, 0), j)),
            pl.BlockSpec((None, kw, cc), lambda i, j: (layer, 0, j)),
            pl.BlockSpec((None, 1, cc), lambda i, j: (layer, 0, j)),
            pl.BlockSpec((None, 1, c), lambda i, j: (layer, 0, 0)),
            pl.BlockSpec((None, 1, c), lambda i, j: (layer, 0, 0)),
        ],
        out_specs=pl.BlockSpec((tt, c), lambda i, j: (i, 0)),
        out_shape=jax.ShapeDtypeStruct((m, c), BF16),
        scratch_shapes=[pltpu.VMEM((tt + halo, cc), F32), pltpu.VMEM((7, tt + halo - 8, cc), F32),
                        pltpu.VMEM((c // cc, tt, cc), F32)],
        name="conv_ln_silu",
        compiler_params=pltpu.CompilerParams(
            dimension_semantics=("arbitrary", "arbitrary"),
            vmem_limit_bytes=_vmem_limit(3 * tt * c * 4 + 14 * (tt + halo) * cc * 4)),
    )(u, u, conv_w, conv_b, ln_g, ln_b)


def _gla_kernel(q_ref, k_ref, v_ref, r_ref, a_ref, wup_ref, ba_ref, g_ref, o_ref,
                state, qe_s, qi_s, ki_s, kd_s, dec_s, obuf, *, rank, chunk):
    tb, dk = q_ref.shape
    dv = v_ref.shape[1]
    hk, hv = dk // GLA_HEADS, dv // GLA_HEADS
    nchunks = tb // chunk

    @pl.when(pl.program_id(0) == 0)
    def _():
        state[...] = jnp.zeros_like(state)

    lane = lax.broadcasted_iota(jnp.int32, a_ref.shape, 1)
    a_low = jnp.where(lane < rank, a_ref[...], 0.0)
    z = _dot_f32(a_low, wup_ref[...]) + ba_ref[...]
    la = (jnp.minimum(z, 0.0) - jnp.log1p(jnp.exp(-jnp.abs(z)))) * (1.0 / GLA_GATE_TEMP)
    la3 = _split3(la)

    rr = lax.broadcasted_iota(jnp.int32, (tb, tb), 0)
    cc = lax.broadcasted_iota(jnp.int32, (tb, tb), 1)
    sh = chunk.bit_length() - 1
    assert chunk == 1 << sh
    same = (rr >> sh) == (cc >> sh)
    tril = jnp.where(same & (cc <= rr), 1.0, 0.0).astype(BF16)
    ones_blk = jnp.where(same, 1.0, 0.0).astype(BF16)
    b = _dot(tril, la3[0]) + (_dot(tril, la3[1]) + _dot(tril, la3[2]))
    bl = _dot(ones_blk, la3[0]) + (_dot(ones_blk, la3[1]) + _dot(ones_blk, la3[2]))

    sr = lax.broadcasted_iota(jnp.int32, (tb, nchunks * LANES), 0)
    sc = lax.broadcasted_iota(jnp.int32, (tb, nchunks * LANES), 1)
    sel = jnp.where((sr >> sh) == (sc >> (LANES.bit_length() - 1)), 1.0, 0.0).astype(BF16)
    tn_dims = (((0,), (0,)), ((), ()))
    tot_t = (lax.dot_general(la3[0], sel, tn_dims, preferred_element_type=F32)
             + (lax.dot_general(la3[1], sel, tn_dims, preferred_element_type=F32)
                + lax.dot_general(la3[2], sel, tn_dims, preferred_element_type=F32)))
    dec_s[...] = jnp.exp(tot_t)

    q = q_ref[...].astype(F32) * (float(hk) ** -0.5)
    k = k_ref[...].astype(F32)
    half = 0.5 * bl
    qe_s[...] = (q * jnp.exp(b)).astype(BF16)
    qi_s[...] = (q * jnp.exp(b - half)).astype(BF16)
    ki_s[...] = (k * jnp.exp(half - b)).astype(BF16)
    kd_s[...] = (k * jnp.exp(bl - b)).astype(BF16)

    ri = lax.broadcasted_iota(jnp.int32, (chunk, chunk), 0)
    ci = lax.broadcasted_iota(jnp.int32, (chunk, chunk), 1)
    causal = ci <= ri
    nt_dims = (((1,), (1,)), ((), ()))
    for c in range(nchunks):
        rows = pl.ds(c * chunk, chunk)
        for h in range(GLA_HEADS):
            ks = pl.ds(h * hk, hk)
            vs = pl.ds(h * hv, hv)
            vch = v_ref[rows, vs].astype(BF16)
            scores = lax.dot_general(qi_s[rows, ks], ki_s[rows, ks], nt_dims, preferred_element_type=F32)
            scores = jnp.where(causal, scores, 0.0).astype(BF16)
            s_old = state[h]
            o = _dot(qe_s[rows, ks], s_old.astype(BF16)) + _dot(scores, vch)
            obuf[rows, vs] = o
            dcol = dec_s[ks, pl.ds(c * LANES, LANES)]
            decay = jnp.concatenate([dcol] * (hv // LANES), axis=1) if hv >= LANES else dcol[:, :hv]
            state[h] = s_old * decay + lax.dot_general(kd_s[rows, ks], vch, tn_dims, preferred_element_type=F32)

    for h in range(GLA_HEADS):
        vs = pl.ds(h * hv, hv)
        o = obuf[:, vs]
        ms = jnp.mean(o * o, axis=-1, keepdims=True)
        on = o * lax.rsqrt(ms + LN_EPS) * g_ref[:, vs]
        r = r_ref[:, vs].astype(F32)
        o_ref[:, vs] = (on * (r * jax.nn.sigmoid(r))).astype(o_ref.dtype)


def _gla(h2, a_low, wup_pad, b_a, norm_g, layer, *, dk, dv, rank):
    m = h2.shape[0]
    tb = _pick(m, 256, GLA_CHUNK)
    assert tb % GLA_CHUNK == 0 and dv == 2 * dk
    nchunks = tb // GLA_CHUNK
    hk, hv = dk // GLA_HEADS, dv // GLA_HEADS
    kernel = functools.partial(_gla_kernel, rank=rank, chunk=GLA_CHUNK)
    return pl.pallas_call(
        kernel, grid=(m // tb,),
        in_specs=[
            pl.BlockSpec((tb, dk), lambda i: (i, 0)),
            pl.BlockSpec((tb, dk), lambda i: (i, 1)),
            pl.BlockSpec((tb, dv), lambda i: (i, 1)),
            pl.BlockSpec((tb, dv), lambda i: (i, 2)),
            pl.BlockSpec((tb, LANES), lambda i: (i, 0)),
            pl.BlockSpec((None, LANES, dk), lambda i: (layer, 0, 0)),
            pl.BlockSpec((None, 1, dk), lambda i: (layer, 0, 0)),
            pl.BlockSpec((None, 1, dv), lambda i: (layer, 0, 0)),
        ],
        out_specs=pl.BlockSpec((tb, dv), lambda i: (i, 0)),
        out_shape=jax.ShapeDtypeStruct((m, dv), BF16),
        scratch_shapes=[
            pltpu.VMEM((GLA_HEADS, hk, hv), F32),
            pltpu.VMEM((tb, dk), BF16), pltpu.VMEM((tb, dk), BF16),
            pltpu.VMEM((tb, dk), BF16), pltpu.VMEM((tb, dk), BF16),
            pltpu.VMEM((dk, nchunks * LANES), F32),
            pltpu.VMEM((tb, dv), F32),
        ],
        name="gla",
        compiler_params=pltpu.CompilerParams(
            dimension_semantics=("arbitrary",), vmem_limit_bytes=_vmem_limit(48 << 20)),
    )(h2, h2, h2, h2, a_low, wup_pad, b_a, norm_g)


def _pack_halves(x):
    n = x.shape[1] // 2
    hi = lax.bitcast_convert_type(x[:, :n].astype(BF16).astype(F32), jnp.uint32)
    lo = lax.bitcast_convert_type(x[:, n:].astype(BF16).astype(F32), jnp.uint32)
    return hi | (lo >> 16)


def _unpack_halves(w):
    hi = lax.bitcast_convert_type(w & jnp.uint32(0xFFFF0000), F32)
    lo = lax.bitcast_convert_type(w << 16, F32)
    return hi, lo


def _ln_kernel(z_ref, g_ref, b_ref, *rest, with_router):
    if with_router:
        wr_ref, br_ref, x_ref, xp_ref, lg_ref = rest
    else:
        x_ref, xb_ref = rest
    z = z_ref[...]
    mu = jnp.mean(z, axis=-1, keepdims=True)
    d = z - mu
    var = jnp.mean(d * d, axis=-1, keepdims=True)
    x = d * lax.rsqrt(var + LN_EPS) * g_ref[...] + b_ref[...]
    x_ref[...] = x
    if with_router:
        xp_ref[...] = _pack_halves(x)
        lg_ref[...] = _dot_f32(x, wr_ref[...]) + br_ref[...]
    else:
        xb_ref[...] = x.astype(BF16)


def _layer_norm(z, ln_g, ln_b, idx, router=None):
    m, d = z.shape
    tr = _pick(m, 512, 8)
    in_specs = [
        pl.BlockSpec((tr, d), lambda i: (i, 0)),
        pl.BlockSpec((None, 1, d), lambda i: (idx, 0, 0)),
        pl.BlockSpec((None, 1, d), lambda i: (idx, 0, 0)),
    ]
    args = [z, ln_g, ln_b]
    if router is None:
        out_specs = [pl.BlockSpec((tr, d), lambda i: (i, 0)), pl.BlockSpec((tr, d), lambda i: (i, 0))]
        out_shape = [jax.ShapeDtypeStruct((m, d), F32), jax.ShapeDtypeStruct((m, d), BF16)]
    else:
        in_specs += [pl.BlockSpec((d, LANES), lambda i: (0, 0)), pl.BlockSpec((1, LANES), lambda i: (0, 0))]
        args += list(router)
        out_specs = [pl.BlockSpec((tr, d), lambda i: (i, 0)), pl.BlockSpec((tr, d // 2), lambda i: (i, 0)),
                     pl.BlockSpec((tr, LANES), lambda i: (i, 0))]
        out_shape = [jax.ShapeDtypeStruct((m, d), F32), jax.ShapeDtypeStruct((m, d // 2), jnp.uint32),
                     jax.ShapeDtypeStruct((m, LANES), F32)]
    return pl.pallas_call(
        functools.partial(_ln_kernel, with_router=router is not None),
        grid=(m // tr,), in_specs=in_specs, out_specs=out_specs, out_shape=out_shape,
        name="layer_norm_router" if router is not None else "layer_norm",
        compiler_params=pltpu.CompilerParams(
            dimension_semantics=("arbitrary",), vmem_limit_bytes=_vmem_limit(tr * d * 36)),
    )(*args)


def _route_kernel(lg_ref, idx_ref, w_ref, rank_ref, cnt_ref, *, n_experts):
    per = n_experts // N_GROUPS
    logit = [lg_ref[e] for e in range(n_experts)]
    mx = functools.reduce(jnp.maximum, logit)
    ex = [jnp.exp(l - mx) for l in logit]
    tot = functools.reduce(lambda a, c: a + c, ex)
    p = [e / tot for e in ex]

    def top2_sum(vals):
        a, bb, c, d = vals
        hi1, lo1 = jnp.maximum(a, bb), jnp.minimum(a, bb)
        hi2, lo2 = jnp.maximum(c, d), jnp.minimum(c, d)
        first = jnp.maximum(hi1, hi2)
        second = jnp.maximum(jnp.minimum(hi1, hi2), jnp.maximum(lo1, lo2))
        return first + second

    assert per == 4
    gs = [top2_sum(p[g * per:(g + 1) * per]) for g in range(N_GROUPS)]
    best = jnp.zeros_like(gs[0], dtype=jnp.int32)
    best_v = gs[0]
    for g in range(1, N_GROUPS):
        upd = gs[g] > best_v
        best = jnp.where(upd, g, best)
        best_v = jnp.where(upd, gs[g], best_v)
    vals = []
    for k in range(per):
        v = p[k]
        for g in range(1, N_GROUPS):
            v = jnp.where(best == g, p[g * per + k], v)
        vals.append(v)
    i1 = jnp.zeros_like(best)
    v1 = vals[0]
    for k in range(1, per):
        upd = vals[k] > v1
        i1 = jnp.where(upd, k, i1)
        v1 = jnp.where(upd, vals[k], v1)
    i2 = jnp.full_like(best, -1)
    v2 = jnp.full_like(v1, -1.0)
    for k in range(per):
        upd = (i1 != k) & (vals[k] > v2)
        i2 = jnp.where(upd, k, i2)
        v2 = jnp.where(upd, vals[k], v2)
    den = v1 + v2
    picks = (best * per + i1, best * per + i2)
    idx_ref[0] = picks[0]
    idx_ref[1] = picks[1]
    w_ref[0] = v1 / den
    w_ref[1] = v2 / den

    rows, lanes = best.shape
    li = lax.broadcasted_iota(jnp.int32, (lanes, lanes), 0)
    lj = lax.broadcasted_iota(jnp.int32, (lanes, lanes), 1)
    upper = jnp.where(li <= lj, 1.0, 0.0).astype(BF16)
    ones = jnp.ones((lanes, lanes), BF16)
    ri = lax.broadcasted_iota(jnp.int32, (rows, rows), 0)
    rj = lax.broadcasted_iota(jnp.int32, (rows, rows), 1)
    before = jnp.where(rj < ri, 1.0, 0.0).astype(BF16)
    ranks = [jnp.zeros((rows, lanes), F32), jnp.zeros((rows, lanes), F32)]
    for e in range(n_experts):
        base = jnp.zeros((1, lanes), F32)
        for s in range(2):
            hit = picks[s] == e
            ind = jnp.where(hit, 1.0, 0.0).astype(BF16)
            in_row = _dot(ind, upper)
            row_tot = _dot(ind, ones)
            row_off = _dot(before, row_tot.astype(BF16))
            ranks[s] = jnp.where(hit, in_row + row_off + (base - 1.0), ranks[s])
            base = base + (row_off + row_tot)[rows - 1:rows, :]
        cnt_ref[pl.ds(e, 1), :] = base.astype(jnp.int32)
    rank_ref[0] = ranks[0].astype(jnp.int32)
    rank_ref[1] = ranks[1].astype(jnp.int32)


def _route(logits_t):
    n_experts, rows, lanes = logits_t.shape
    return pl.pallas_call(
        functools.partial(_route_kernel, n_experts=n_experts),
        out_shape=[jax.ShapeDtypeStruct((2, rows, lanes), jnp.int32),
                   jax.ShapeDtypeStruct((2, rows, lanes), F32),
                   jax.ShapeDtypeStruct((2, rows, lanes), jnp.int32),
                   jax.ShapeDtypeStruct((n_experts, lanes), jnp.int32)],
        name="route",
    )(logits_t)


def _dispatch_kernel(pos_ref, x_hbm, init_hbm, o_hbm, sem, *, m, chunk):
    del init_hbm
    i = pl.program_id(0)
    last = pl.num_programs(0) - 1

    def row_copy(src, dst, slot):
        return pltpu.make_async_copy(x_hbm.at[pl.ds(src, 1), :], o_hbm.at[pl.ds(dst, 1), :], sem.at[slot])

    def issue(r, carry):
        t = i * chunk + r
        row_copy(t, pos_ref[t], i % 2).start()
        row_copy(t, pos_ref[m + t], i % 2).start()
        return carry
    lax.fori_loop(0, chunk, issue, 0, unroll=8)

    def drain(slot):
        def body(r, carry):
            row_copy(0, 0, slot).wait()
            row_copy(0, 0, slot).wait()
            return carry
        lax.fori_loop(0, chunk, body, 0, unroll=8)

    @pl.when(i > 0)
    def _():
        drain((i + 1) % 2)

    @pl.when(i == last)
    def _():
        drain(i % 2)


def _dispatch_rows(xp, pos, npad):
    m, n = xp.shape
    chunk = _pick(m, 128, 8)
    return pl.pallas_call(
        functools.partial(_dispatch_kernel, m=m, chunk=chunk),
        grid_spec=pltpu.PrefetchScalarGridSpec(
            num_scalar_prefetch=1, grid=(m // chunk,),
            in_specs=[pl.BlockSpec(memory_space=pl.ANY), pl.BlockSpec(memory_space=pl.ANY)],
            out_specs=pl.BlockSpec(memory_space=pl.ANY),
            scratch_shapes=[pltpu.SemaphoreType.DMA((2,))]),
        out_shape=jax.ShapeDtypeStruct((npad, n), xp.dtype),
        input_output_aliases={2: 0},
        name="expert_dispatch",
        compiler_params=pltpu.CompilerParams(dimension_semantics=("arbitrary",)),
    )(pos, xp, jnp.zeros((npad, n), xp.dtype))


def _new_expert(te_ref, t):
    return (t == 0) | (te_ref[t] != te_ref[jnp.maximum(t - 1, 0)])


def _expert_up_kernel(te_ref, tr_ref, nt_ref, x_ref, wg_ref, wu_ref, o_ref, wg_s, wu_s):
    t = pl.program_id(1)
    live = t < nt_ref[0]
    n_valid = tr_ref[t]
    tm, half_d = x_ref.shape
    half_m = tm // 2

    @pl.when(live & _new_expert(te_ref, t))
    def _():
        wg_s[...] = wg_ref[...].astype(BF16)
        wu_s[...] = wu_ref[...].astype(BF16)

    def compute(nr):
        x_hi, x_lo = _unpack_halves(x_ref[pl.ds(0, nr), :])
        x_hi, x_lo = x_hi.astype(BF16), x_lo.astype(BF16)
        top, bot = pl.ds(0, half_d), pl.ds(half_d, half_d)
        g = _dot(x_hi, wg_s[top, :]) + _dot(x_lo, wg_s[bot, :])
        u = _dot(x_hi, wu_s[top, :]) + _dot(x_lo, wu_s[bot, :])
        o_ref[pl.ds(0, nr), :] = (g * jax.nn.sigmoid(g) * u).astype(o_ref.dtype)

    @pl.when(live & (n_valid > half_m))
    def _():
        compute(tm)

    @pl.when(live & (n_valid <= half_m))
    def _():
        compute(half_m)
        o_ref[pl.ds(half_m, half_m), :] = jnp.zeros((half_m, o_ref.shape[1]), o_ref.dtype)

    @pl.when(jnp.logical_not(live))
    def _():
        o_ref[...] = jnp.zeros_like(o_ref)


def _expert_up(xg, w_gate, w_up, layer, tile_e, tile_rows, n_tiles, *, tm):
    npad, half_d = xg.shape
    d = 2 * half_d
    f = w_gate.shape[-1]
    tf = _pick(f, 512)
    w_spec = pl.BlockSpec((None, None, d, tf), lambda j, t, te, tr, nt: (layer, te[t], 0, j))
    return pl.pallas_call(
        _expert_up_kernel,
        grid_spec=pltpu.PrefetchScalarGridSpec(
            num_scalar_prefetch=3, grid=(f // tf, npad // tm),
            in_specs=[pl.BlockSpec((tm, half_d), lambda j, t, te, tr, nt: (t, 0)), w_spec, w_spec],
            out_specs=pl.BlockSpec((tm, tf), lambda j, t, te, tr, nt: (t, j)),
            scratch_shapes=[pltpu.VMEM((d, tf), BF16), pltpu.VMEM((d, tf), BF16)]),
        out_shape=jax.ShapeDtypeStruct((npad, f), BF16),
        name="expert_up",
        compiler_params=pltpu.CompilerParams(
            dimension_semantics=("arbitrary", "arbitrary"),
            vmem_limit_bytes=_vmem_limit(4 * d * tf * 4 + 2 * d * tf * 2 + 5 * tm * d * 2 + 8 * tm * tf * 4)),
    )(tile_e, tile_rows, n_tiles, xg, w_gate, w_up)


def _expert_down_kernel(te_ref, tr_ref, nt_ref, h_ref, wd_ref, o_ref, wd_s):
    t = pl.program_id(1)
    live = t < nt_ref[0]
    n_valid = tr_ref[t]
    tm = h_ref.shape[0]
    half_m = tm // 2

    @pl.when(live & _new_expert(te_ref, t))
    def _():
        wd_s[...] = wd_ref[...].astype(BF16)

    def compute(nr):
        o_ref[pl.ds(0, nr), :] = _pack_halves(_dot(h_ref[pl.ds(0, nr), :], wd_s[...]))

    @pl.when(live & (n_valid > half_m))
    def _():
        compute(tm)

    @pl.when(live & (n_valid <= half_m))
    def _():
        compute(half_m)
        o_ref[pl.ds(half_m, half_m), :] = jnp.zeros((half_m, o_ref.shape[1]), o_ref.dtype)

    @pl.when(jnp.logical_not(live))
    def _():
        o_ref[...] = jnp.zeros_like(o_ref)


def _expert_down(hg, w_down, layer, tile_e, tile_rows, n_tiles, *, tm, tn):
    npad, f = hg.shape
    d = w_down.shape[-1]
    return pl.pallas_call(
        _expert_down_kernel,
        grid_spec=pltpu.PrefetchScalarGridSpec(
            num_scalar_prefetch=3, grid=(d // tn, npad // tm),
            in_specs=[pl.BlockSpec((tm, f), lambda j, t, te, tr, nt: (t, 0)),
                      pl.BlockSpec((None, None, f, tn), lambda j, t, te, tr, nt: (layer, te[t], 0, j))],
            out_specs=pl.BlockSpec((tm, tn // 2), lambda j, t, te, tr, nt: (t, j)),
            scratch_shapes=[pltpu.VMEM((f, tn), BF16)]),
        out_shape=jax.ShapeDtypeStruct((npad, d // 2), jnp.uint32),
        name="expert_down",
        compiler_params=pltpu.CompilerParams(
            dimension_semantics=("arbitrary", "arbitrary"),
            vmem_limit_bytes=_vmem_limit(2 * f * tn * 4 + f * tn * 2 + 2 * tm * f * 2 + 6 * tm * tn * 4)),
    )(tile_e, tile_rows, n_tiles, hg, w_down)


def _combine_ln_kernel(pos_ref, y_hbm, x_ref, w0_ref, w1_ref, g_ref, b_ref, o_ref, ob_ref, buf, sem,
                       *, alpha, m, tn):
    i = pl.program_id(0)
    tc, d = x_ref.shape

    def row_copy(s, r, p):
        return pltpu.make_async_copy(y_hbm.at[pl.ds(p, 1), :], buf.at[s, pl.ds(r, 1), :], sem)

    def issue(r, carry):
        row_copy(0, r, pos_ref[i * tc + r]).start()
        row_copy(1, r, pos_ref[m + i * tc + r]).start()
        return carry
    lax.fori_loop(0, tc, issue, 0, unroll=8)

    def drain(r, carry):
        row_copy(0, r, 0).wait()
        row_copy(1, r, 0).wait()
        return carry
    lax.fori_loop(0, tc, drain, 0, unroll=8)

    w0, w1 = w0_ref[...], w1_ref[...]
    parts = []
    for j in range(d // tn):
        cs = pl.ds(j * (tn // 2), tn // 2)
        hi0, lo0 = _unpack_halves(buf[0, :, cs])
        hi1, lo1 = _unpack_halves(buf[1, :, cs])
        parts += [w0 * hi0 + w1 * hi1, w0 * lo0 + w1 * lo1]
    z = alpha * x_ref[...] + jnp.concatenate(parts, axis=1)
    mu = jnp.mean(z, axis=-1, keepdims=True)
    dz = z - mu
    var = jnp.mean(dz * dz, axis=-1, keepdims=True)
    x = dz * lax.rsqrt(var + LN_EPS) * g_ref[...] + b_ref[...]
    o_ref[...] = x
    ob_ref[...] = x.astype(BF16)


def _combine_ln(yg, pos, ewts, x, ln_g, ln_b, idx, *, alpha, tn):
    m, d = x.shape
    tc = _pick(m, 256, 8)
    row = lambda i, pos: (i, 0)
    par = lambda i, pos: (idx, 0, 0)
    return pl.pallas_call(
        functools.partial(_combine_ln_kernel, alpha=alpha, m=m, tn=tn),
        grid_spec=pltpu.PrefetchScalarGridSpec(
            num_scalar_prefetch=1, grid=(m // tc,),
            in_specs=[pl.BlockSpec(memory_space=pl.ANY),
                      pl.BlockSpec((tc, d), row),
                      pl.BlockSpec((None, tc, 1), lambda i, pos: (0, i, 0)),
                      pl.BlockSpec((None, tc, 1), lambda i, pos: (1, i, 0)),
                      pl.BlockSpec((None, 1, d), par), pl.BlockSpec((None, 1, d), par)],
            out_specs=[pl.BlockSpec((tc, d), row), pl.BlockSpec((tc, d), row)],
            scratch_shapes=[pltpu.VMEM((2, tc, d // 2), jnp.uint32), pltpu.SemaphoreType.DMA(())]),
        out_shape=[jax.ShapeDtypeStruct((m, d), F32), jax.ShapeDtypeStruct((m, d), BF16)],
        name="combine_ln",
        compiler_params=pltpu.CompilerParams(
            dimension_semantics=("arbitrary",), vmem_limit_bytes=_vmem_limit(tc * d * 48)),
    )(pos, yg, x, ewts, ewts, ln_g, ln_b)


def _expert_schedule(eidx, rank, counts, tm, n_tiles_max):
    n_experts = counts.shape[0]
    tiles = (counts + tm - 1) // tm
    tile_end = jnp.cumsum(tiles)
    tile_start = tile_end - tiles
    experts = jnp.arange(n_experts, dtype=jnp.int32)
    pick = eidx[:, :, None] == experts
    pos = rank + jnp.sum(jnp.where(pick, tile_start * tm, 0), axis=-1)
    n_tiles = tile_end[-1]
    tile_ids = jnp.arange(n_tiles_max, dtype=jnp.int32)
    tile_e = jnp.sum(tile_end[None, :] <= jnp.minimum(tile_ids, n_tiles - 1)[:, None], axis=-1)
    mine = tile_e[:, None] == experts
    filled = jnp.sum(jnp.where(mine, counts - (tile_ids[:, None] - tile_start) * tm, 0), axis=-1)
    tile_rows = jnp.clip(filled, 0, tm)
    i32 = lambda a: a.astype(jnp.int32)
    return i32(pos).reshape(-1), i32(tile_e), i32(tile_rows), i32(n_tiles).reshape(1)


def kernel(x, p, w_in, b_branch_gate, w_gla_gate_up, b_gla_gate, conv_w, conv_b, conv_ln_g, conv_ln_b, w_conv_out, gla_norm_g, w_gla_out, w_out, ln_g, ln_b, w_router, b_router, w_exp_gate, w_exp_up, w_exp_down, w_ple, w_ple_gate):
    bsz, seq, d = x.shape
    assert bsz == 1
    depth = w_in.shape[0]
    m = bsz * seq
    conv_ch = conv_w.shape[-1]
    rank, dk = w_gla_gate_up.shape[1], w_gla_gate_up.shape[2]
    dv = gla_norm_g.shape[-1]
    n_experts = w_router.shape[1]
    off_b = conv_ch
    off_q = 2 * conv_ch
    off_alpha = off_q + 2 * dk + 2 * dv
    off_gate = off_alpha + rank
    alpha = (2.0 * depth) ** 0.25
    tm_e = 512 if m >= 4096 else 64
    n_tiles_max = (2 * m) // tm_e + n_experts
    tn_down = _pick(d, 2048)

    xf = x.reshape(m, d)
    xb = xf.astype(BF16)
    pb = p.reshape(depth, m, p.shape[-1]).astype(BF16)
    b_gate = b_branch_gate.reshape(depth, 1, 2 * d)
    wup_pad = jnp.zeros((depth, LANES, dk), F32).at[:, :rank, :].set(w_gla_gate_up)
    b_a = b_gla_gate.reshape(depth, 1, dk)
    conv_b3 = conv_b.reshape(depth, 1, conv_ch)
    cln_g = conv_ln_g.reshape(depth, 1, conv_ch)
    cln_b = conv_ln_b.reshape(depth, 1, conv_ch)
    gnorm = gla_norm_g.reshape(depth, 1, dv)
    ln_g3 = ln_g.reshape(depth * 3, 1, d)
    ln_b3 = ln_b.reshape(depth * 3, 1, d)
    wr_pad = jnp.zeros((d, LANES), F32).at[:, :n_experts].set(w_router)
    br_pad = jnp.zeros((1, LANES), F32).at[0, :n_experts].set(b_router)

    tm = _pick(m, 1024, 8)
    for i in range(depth):
        (u,) = _fused_matmul(
            [xb], [(0, w_in, i, 0), (0, w_in, i, off_b)], [],
            lambda ds, ex: [ds[0] * jax.nn.sigmoid(ds[1])], [F32],
            n_cols=conv_ch, tm=tm, tn=_pick(conv_ch, 256), name="in_proj_glu")
        (h2,) = _fused_matmul(
            [xb], [(0, w_in, i, off_q)], [], lambda ds, ex: [ds[0]], [F32],
            n_cols=2 * dk + 2 * dv, tm=tm, tn=_pick(dk, 512), name="in_proj_qkvr")
        (a_low,) = _fused_matmul(
            [xb], [(0, w_in, i, off_alpha)], [], lambda ds, ex: [ds[0]], [F32],
            n_cols=LANES, tm=tm, tn=LANES, name="in_proj_alpha")
        (gates,) = _fused_matmul(
            [xb], [(0, w_in, i, off_gate)], [('row', b_gate, i, 0)],
            lambda ds, ex: [jax.nn.sigmoid(ds[0] + ex[0])], [BF16],
            n_cols=2 * d, tm=tm, tn=_pick(d, 512), name="in_proj_gates")

        uc = _conv_ln_silu(u, conv_w, conv_b3, cln_g, cln_b, i)
        og = _gla(h2, a_low, wup_pad, b_a, gnorm, i, dk=dk, dv=dv, rank=rank)

        tn_o = _pick(d, 512)
        (merged,) = _fused_matmul(
            [uc, og], [(0, w_conv_out, i, 0), (1, w_gla_out, i, 0)],
            [('tile', gates, None, 0), ('tile', gates, None, d)],
            lambda ds, ex: [ex[0].astype(F32) * ds[0] + ex[1].astype(F32) * ds[1]], [BF16],
            n_cols=d, tm=tm, tn=tn_o, name="branch_out_merge")
        (z1,) = _fused_matmul(
            [merged], [(0, w_out, i, 0)], [('tile', xf, None, 0)],
            lambda ds, ex: [alpha * ex[0] + ds[0]], [F32],
            n_cols=d, tm=tm, tn=tn_o, name="mixer_out")
        x1, x1p, logits = _layer_norm(z1, ln_g3, ln_b3, 3 * i, router=(wr_pad, br_pad))

        logits_t = logits[:, :n_experts].T.reshape(n_experts, m // LANES, LANES)
        eidx, ewts, pair_rank, counts = _route(logits_t)
        pos, tile_e, tile_rows, n_tiles = _expert_schedule(
            eidx.reshape(2, m), pair_rank.reshape(2, m), counts[:, 0], tm_e, n_tiles_max)
        xg = _dispatch_rows(x1p, pos, n_tiles_max * tm_e)
        hg = _expert_up(xg, w_exp_gate, w_exp_up, i, tile_e, tile_rows, n_tiles, tm=tm_e)
        yg = _expert_down(hg, w_exp_down, i, tile_e, tile_rows, n_tiles, tm=tm_e, tn=tn_down)
        x2, x2b = _combine_ln(yg, pos, ewts.reshape(2, m, 1), x1, ln_g3, ln_b3, 3 * i + 1,
                              alpha=alpha, tn=tn_down)

        (z3,) = _fused_matmul(
            [x2b, pb[i]], [(0, w_ple_gate, i, 0), (1, w_ple, i, 0)], [('tile', x2, None, 0)],
            lambda ds, ex: [alpha * ex[0] + jax.nn.sigmoid(ds[0]) * ds[1]], [F32],
            n_cols=d, tm=tm, tn=tn_o, name="ple")
        xf, xb = _layer_norm(z3, ln_g3, ln_b3, 3 * i + 2)

    return xf.reshape(bsz, seq, d)
```

```python
import functools

import jax
import jax.numpy as jnp
from jax import lax
from jax.experimental import pallas as pl
from jax.experimental.pallas import tpu as pltpu

F32 = jnp.float32
BF16 = jnp.bfloat16

GLA_HEADS = 4
GLA_GATE_TEMP = 16.0
GLA_CHUNK = 64
N_GROUPS = 4
LN_EPS = 1e-5
LANES = 128
V7X_SCOPED_VMEM_BYTES = 60000 * 1024


def _pick(n, pref, mult=LANES):
    if n <= pref:
        return n
    t = (pref // mult) * mult
    while t >= mult:
        if n % t == 0:
            return t
        t -= mult
    return n


def _vmem_limit(nbytes):
    return int(min(max(nbytes * 5 // 4 + (4 << 20), 16 << 20), V7X_SCOPED_VMEM_BYTES))


def _split3(x):
    hi = x.astype(BF16)
    r1 = x - hi.astype(F32)
    mid = r1.astype(BF16)
    lo = (r1 - mid.astype(F32)).astype(BF16)
    return hi, mid, lo


def _dot(a, b):
    return jnp.dot(a, b, preferred_element_type=F32)


def _dot_f32(a, b):
    ah, am, _ = _split3(a)
    bh, bm, _ = _split3(b)
    return _dot(ah, bh) + (_dot(ah, bm) + _dot(am, bh))


NT_DIMS = (((1,), (1,)), ((), ()))


def _fused_matmul_kernel(*refs, n_acts, dot_act, forms, n_extras, n_outs, epilogue):
    n_w = sum(2 if shift else 1 for _, shift in forms)
    act_refs = refs[:n_acts]
    w_refs = list(refs[n_acts:n_acts + n_w])
    ex_refs = refs[n_acts + n_w:n_acts + n_w + n_extras]
    out_refs = refs[n_acts + n_w + n_extras:n_acts + n_w + n_extras + n_outs]
    w_bf16 = refs[n_acts + n_w + n_extras + n_outs:]

    @pl.when(pl.program_id(1) == 0)
    def _():
        for s, (_, shift) in zip(w_bf16, forms):
            w_ref = w_refs.pop(0)
            if shift:
                tn = s.shape[0]
                s[pl.ds(0, tn - shift), :] = w_ref[pl.ds(shift, tn - shift), :].astype(BF16)
                s[pl.ds(tn - shift, shift), :] = w_refs.pop(0)[...].astype(BF16)
            else:
                s[...] = w_ref[...].astype(BF16)

    dots = []
    for ai, s, (output_major, _) in zip(dot_act, w_bf16, forms):
        a = act_refs[ai][...]
        dots.append(lax.dot_general(a, s[...], NT_DIMS, preferred_element_type=F32) if output_major
                    else _dot(a, s[...]))
    outs = epilogue(dots, [e[...] for e in ex_refs])
    for o_ref, o in zip(out_refs, outs):
        o_ref[...] = o.astype(o_ref.dtype)


def _fused_matmul(acts, dots, extras, epilogue, out_dtypes, *, n_cols, tm, tn, name):
    m = acts[0].shape[0]
    assert m % tm == 0 and n_cols % tn == 0
    grid = (n_cols // tn, m // tm)
    in_specs, args, scratch, forms = [], [], [], []
    nbytes = 0
    for a in acts:
        k = a.shape[1]
        in_specs.append(pl.BlockSpec((tm, k), lambda j, i: (i, 0)))
        args.append(a)
        nbytes += 2 * tm * k * a.dtype.itemsize
    for (_, w, layer, off, output_major) in dots:
        shift = off % tn
        if output_major:
            k = w.shape[2]
            in_specs.append(pl.BlockSpec((None, tn, k), lambda j, i, layer=layer, cb=off // tn: (layer, cb + j, 0)))
            args.append(w)
            if shift:
                assert shift % 16 == 0 and tn % shift == 0 and (off - shift) % shift == 0
                in_specs.append(pl.BlockSpec(
                    (None, shift, k),
                    lambda j, i, layer=layer, cb=off // tn, r=tn // shift: (layer, (cb + j + 1) * r, 0)))
                args.append(w)
                nbytes += 2 * shift * k * 4
            scratch.append(pltpu.VMEM((tn, k), BF16))
        else:
            assert shift == 0
            k = w.shape[1]
            in_specs.append(pl.BlockSpec((None, k, tn), lambda j, i, layer=layer, cb=off // tn: (layer, 0, cb + j)))
            args.append(w)
            scratch.append(pltpu.VMEM((k, tn), BF16))
        forms.append((output_major, shift))
        nbytes += 2 * k * tn * 4 + k * tn * 2
    for (kind, arr, layer, off) in extras:
        assert off % tn == 0
        if kind == 'tile':
            in_specs.append(pl.BlockSpec((tm, tn), lambda j, i, cb=off // tn: (i, cb + j)))
            nbytes += 2 * tm * tn * arr.dtype.itemsize
        else:
            in_specs.append(pl.BlockSpec((None, 1, tn), lambda j, i, layer=layer, cb=off // tn: (layer, 0, cb + j)))
        args.append(arr)
    out_specs = [pl.BlockSpec((tm, tn), lambda j, i: (i, j)) for _ in out_dtypes]
    out_shape = [jax.ShapeDtypeStruct((m, n_cols), dt) for dt in out_dtypes]
    nbytes += sum(2 * tm * tn * jnp.dtype(dt).itemsize for dt in out_dtypes)
    nbytes += (len(dots) + 3) * tm * tn * 4
    kernel = functools.partial(
        _fused_matmul_kernel, n_acts=len(acts), dot_act=tuple(d[0] for d in dots), forms=tuple(forms),
        n_extras=len(extras), n_outs=len(out_dtypes), epilogue=epilogue)
    return pl.pallas_call(
        kernel, grid=grid, in_specs=in_specs, out_specs=out_specs, out_shape=out_shape,
        scratch_shapes=scratch, name=name,
        compiler_params=pltpu.CompilerParams(
            dimension_semantics=("arbitrary", "arbitrary"), vmem_limit_bytes=_vmem_limit(nbytes)),
    )(*args)


def _conv_kernel(u_ref, prev_ref, w_ref, b_ref, g_ref, beta_ref, o_ref, win, shifted, ybuf, *, kw, halo, rows):
    i = pl.program_id(0)
    j = pl.program_id(1)
    tt, cc = u_ref.shape
    win[pl.ds(0, halo), :] = jnp.where(i > 0, prev_ref[...], 0.0)
    win[pl.ds(halo, tt), :] = u_ref[...]
    span = shifted.shape[1]
    for s in range(1, 8):
        shifted[s - 1] = win[pl.ds(s, span), :]
    lead = halo - (kw - 1)
    for r0 in range(0, tt, rows):
        acc = jnp.broadcast_to(b_ref[...], (rows, cc))
        for t in range(kw):
            s = (lead + t) % 8
            start = r0 + lead + t - s
            src = win[pl.ds(start, rows), :] if s == 0 else shifted[s - 1, pl.ds(start, rows), :]
            acc = acc + src * w_ref[pl.ds(t, 1), :]
        ybuf[j, pl.ds(r0, rows), :] = acc

    @pl.when(j == pl.num_programs(1) - 1)
    def _():
        nc = ybuf.shape[0]
        c = nc * cc
        s = jnp.zeros((tt, 1), F32)
        for k in range(nc):
            s = s + jnp.sum(ybuf[k], axis=-1, keepdims=True)
        mu = s * (1.0 / c)
        v = jnp.zeros((tt, 1), F32)
        for k in range(nc):
            d = ybuf[k] - mu
            v = v + jnp.sum(d * d, axis=-1, keepdims=True)
        rstd = lax.rsqrt(v * (1.0 / c) + LN_EPS)
        for k in range(nc):
            cs = slice(k * cc, (k + 1) * cc)
            y = (ybuf[k] - mu) * rstd * g_ref[:, cs] + beta_ref[:, cs]
            o_ref[:, cs] = (y * jax.nn.sigmoid(y)).astype(o_ref.dtype)


def _conv_ln_silu(u, conv_w, conv_b, ln_g, ln_b, layer):
    m, c = u.shape
    kw = conv_w.shape[1]
    halo = -(-(kw - 1) // 8) * 8
    tt = _pick(m, 512, halo)
    cc = _pick(c, 256)
    rows = 64 if tt % 64 == 0 else tt
    kernel = functools.partial(_conv_kernel, kw=kw, halo=halo, rows=rows)
    return pl.pallas_call(
        kernel, grid=(m // tt, c // cc),
        in_specs=[
            pl.BlockSpec((tt, cc), lambda i, j: (i, j)),
            pl.BlockSpec((halo, cc), lambda i, j: (jnp.maximum(i * (tt // halo) - 1, 0), j)),
            pl.BlockSpec((None, kw, cc), lambda i, j: (layer, 0, j)),
            pl.BlockSpec((None, 1, cc), lambda i, j: (layer, 0, j)),
            pl.BlockSpec((None, 1, c), lambda i, j: (layer, 0, 0)),
            pl.BlockSpec((None, 1, c), lambda i, j: (layer, 0, 0)),
        ],
        out_specs=pl.BlockSpec((tt, c), lambda i, j: (i, 0)),
        out_shape=jax.ShapeDtypeStruct((m, c), BF16),
        scratch_shapes=[pltpu.VMEM((tt + halo, cc), F32), pltpu.VMEM((7, tt + halo - 8, cc), F32),
                        pltpu.VMEM((c // cc, tt, cc), F32)],
        name="conv_ln_silu",
        compiler_params=pltpu.CompilerParams(
            dimension_semantics=("arbitrary", "arbitrary"),
            vmem_limit_bytes=_vmem_limit(3 * tt * c * 4 + 14 * (tt + halo) * cc * 4)),
    )(u, u, conv_w, conv_b, ln_g, ln_b)


def _gla_kernel(q_ref, k_ref, v_ref, r_ref, a_ref, wup_ref, ba_ref, g_ref, o_ref,
                state, qe_s, qi_s, ki_s, kd_s, dec_s, obuf, *, rank, chunk):
    tb, dk = q_ref.shape
    dv = v_ref.shape[1]
    hk, hv = dk // GLA_HEADS, dv // GLA_HEADS
    nchunks = tb // chunk

    @pl.when(pl.program_id(0) == 0)
    def _():
        state[...] = jnp.zeros_like(state)

    lane = lax.broadcasted_iota(jnp.int32, a_ref.shape, 1)
    a_low = jnp.where(lane < rank, a_ref[...], 0.0)
    z = _dot_f32(a_low, wup_ref[...]) + ba_ref[...]
    la = (jnp.minimum(z, 0.0) - jnp.log1p(jnp.exp(-jnp.abs(z)))) * (1.0 / GLA_GATE_TEMP)
    la3 = _split3(la)

    rr = lax.broadcasted_iota(jnp.int32, (tb, tb), 0)
    cc = lax.broadcasted_iota(jnp.int32, (tb, tb), 1)
    sh = chunk.bit_length() - 1
    assert chunk == 1 << sh
    same = (rr >> sh) == (cc >> sh)
    tril = jnp.where(same & (cc <= rr), 1.0, 0.0).astype(BF16)
    ones_blk = jnp.where(same, 1.0, 0.0).astype(BF16)
    b = _dot(tril, la3[0]) + (_dot(tril, la3[1]) + _dot(tril, la3[2]))
    bl = _dot(ones_blk, la3[0]) + (_dot(ones_blk, la3[1]) + _dot(ones_blk, la3[2]))

    sr = lax.broadcasted_iota(jnp.int32, (tb, nchunks * LANES), 0)
    sc = lax.broadcasted_iota(jnp.int32, (tb, nchunks * LANES), 1)
    sel = jnp.where((sr >> sh) == (sc >> (LANES.bit_length() - 1)), 1.0, 0.0).astype(BF16)
    tn_dims = (((0,), (0,)), ((), ()))
    tot_t = (lax.dot_general(la3[0], sel, tn_dims, preferred_element_type=F32)
             + (lax.dot_general(la3[1], sel, tn_dims, preferred_element_type=F32)
                + lax.dot_general(la3[2], sel, tn_dims, preferred_element_type=F32)))
    dec_s[...] = jnp.exp(tot_t)

    q = q_ref[...].astype(F32) * (float(hk) ** -0.5)
    k = k_ref[...].astype(F32)
    half = 0.5 * bl
    qe_s[...] = (q * jnp.exp(b)).astype(BF16)
    qi_s[...] = (q * jnp.exp(b - half)).astype(BF16)
    ki_s[...] = (k * jnp.exp(half - b)).astype(BF16)
    kd_s[...] = (k * jnp.exp(bl - b)).astype(BF16)

    ri = lax.broadcasted_iota(jnp.int32, (chunk, chunk), 0)
    ci = lax.broadcasted_iota(jnp.int32, (chunk, chunk), 1)
    causal = ci <= ri
    nt_dims = (((1,), (1,)), ((), ()))
    for c in range(nchunks):
        rows = pl.ds(c * chunk, chunk)
        for h in range(GLA_HEADS):
            ks = pl.ds(h * hk, hk)
            vs = pl.ds(h * hv, hv)
            vch = v_ref[rows, vs].astype(BF16)
            scores = lax.dot_general(qi_s[rows, ks], ki_s[rows, ks], nt_dims, preferred_element_type=F32)
            scores = jnp.where(causal, scores, 0.0).astype(BF16)
            s_old = state[h]
            o = _dot(qe_s[rows, ks], s_old.astype(BF16)) + _dot(scores, vch)
            obuf[rows, vs] = o
            dcol = dec_s[ks, pl.ds(c * LANES, LANES)]
            decay = jnp.concatenate([dcol] * (hv // LANES), axis=1) if hv >= LANES else dcol[:, :hv]
            state[h] = s_old * decay + lax.dot_general(kd_s[rows, ks], vch, tn_dims, preferred_element_type=F32)

    for h in range(GLA_HEADS):
        vs = pl.ds(h * hv, hv)
        o = obuf[:, vs]
        ms = jnp.mean(o * o, axis=-1, keepdims=True)
        on = o * lax.rsqrt(ms + LN_EPS) * g_ref[:, vs]
        r = r_ref[:, vs].astype(F32)
        o_ref[:, vs] = (on * (r * jax.nn.sigmoid(r))).astype(o_ref.dtype)


def _gla(h2, a_low, wup_pad, b_a, norm_g, layer, *, dk, dv, rank):
    m = h2.shape[0]
    tb = _pick(m, 256, GLA_CHUNK)
    assert tb % GLA_CHUNK == 0 and dv == 2 * dk
    nchunks = tb // GLA_CHUNK
    hk, hv = dk // GLA_HEADS, dv // GLA_HEADS
    kernel = functools.partial(_gla_kernel, rank=rank, chunk=GLA_CHUNK)
    return pl.pallas_call(
        kernel, grid=(m // tb,),
        in_specs=[
            pl.BlockSpec((tb, dk), lambda i: (i, 0)),
            pl.BlockSpec((tb, dk), lambda i: (i, 1)),
            pl.BlockSpec((tb, dv), lambda i: (i, 1)),
            pl.BlockSpec((tb, dv), lambda i: (i, 2)),
            pl.BlockSpec((tb, LANES), lambda i: (i, 0)),
            pl.BlockSpec((None, LANES, dk), lambda i: (layer, 0, 0)),
            pl.BlockSpec((None, 1, dk), lambda i: (layer, 0, 0)),
            pl.BlockSpec((None, 1, dv), lambda i: (layer, 0, 0)),
        ],
        out_specs=pl.BlockSpec((tb, dv), lambda i: (i, 0)),
        out_shape=jax.ShapeDtypeStruct((m, dv), BF16),
        scratch_shapes=[
            pltpu.VMEM((GLA_HEADS, hk, hv), F32),
            pltpu.VMEM((tb, dk), BF16), pltpu.VMEM((tb, dk), BF16),
            pltpu.VMEM((tb, dk), BF16), pltpu.VMEM((tb, dk), BF16),
            pltpu.VMEM((dk, nchunks * LANES), F32),
            pltpu.VMEM((tb, dv), F32),
        ],
        name="gla",
        compiler_params=pltpu.CompilerParams(
            dimension_semantics=("arbitrary",), vmem_limit_bytes=_vmem_limit(48 << 20)),
    )(h2, h2, h2, h2, a_low, wup_pad, b_a, norm_g)


def _pack_halves(x):
    n = x.shape[1] // 2
    hi = lax.bitcast_convert_type(x[:, :n].astype(BF16).astype(F32), jnp.uint32)
    lo = lax.bitcast_convert_type(x[:, n:].astype(BF16).astype(F32), jnp.uint32)
    return hi | (lo >> 16)


def _unpack_halves(w):
    hi = lax.bitcast_convert_type(w & jnp.uint32(0xFFFF0000), F32)
    lo = lax.bitcast_convert_type(w << 16, F32)
    return hi, lo


def _ln_kernel(z_ref, g_ref, b_ref, *rest, with_router):
    if with_router:
        wr_ref, br_ref, x_ref, xp_ref, lg_ref = rest
    else:
        x_ref, xb_ref = rest
    z = z_ref[...]
    mu = jnp.mean(z, axis=-1, keepdims=True)
    d = z - mu
    var = jnp.mean(d * d, axis=-1, keepdims=True)
    x = d * lax.rsqrt(var + LN_EPS) * g_ref[...] + b_ref[...]
    x_ref[...] = x
    if with_router:
        xp_ref[...] = _pack_halves(x)
        lg_ref[...] = _dot_f32(x, wr_ref[...]) + br_ref[...]
    else:
        xb_ref[...] = x.astype(BF16)


def _layer_norm(z, ln_g, ln_b, idx, router=None):
    m, d = z.shape
    tr = _pick(m, 512, 8)
    in_specs = [
        pl.BlockSpec((tr, d), lambda i: (i, 0)),
        pl.BlockSpec((None, 1, d), lambda i: (idx, 0, 0)),
        pl.BlockSpec((None, 1, d), lambda i: (idx, 0, 0)),
    ]
    args = [z, ln_g, ln_b]
    if router is None:
        out_specs = [pl.BlockSpec((tr, d), lambda i: (i, 0)), pl.BlockSpec((tr, d), lambda i: (i, 0))]
        out_shape = [jax.ShapeDtypeStruct((m, d), F32), jax.ShapeDtypeStruct((m, d), BF16)]
    else:
        in_specs += [pl.BlockSpec((d, LANES), lambda i: (0, 0)), pl.BlockSpec((1, LANES), lambda i: (0, 0))]
        args += list(router)
        out_specs = [pl.BlockSpec((tr, d), lambda i: (i, 0)), pl.BlockSpec((tr, d // 2), lambda i: (i, 0)),
                     pl.BlockSpec((tr, LANES), lambda i: (i, 0))]
        out_shape = [jax.ShapeDtypeStruct((m, d), F32), jax.ShapeDtypeStruct((m, d // 2), jnp.uint32),
                     jax.ShapeDtypeStruct((m, LANES), F32)]
    return pl.pallas_call(
        functools.partial(_ln_kernel, with_router=router is not None),
        grid=(m // tr,), in_specs=in_specs, out_specs=out_specs, out_shape=out_shape,
        name="layer_norm_router" if router is not None else "layer_norm",
        compiler_params=pltpu.CompilerParams(
            dimension_semantics=("arbitrary",), vmem_limit_bytes=_vmem_limit(tr * d * 36)),
    )(*args)


def _route_kernel(lg_ref, idx_ref, w_ref, rank_ref, cnt_ref, *, n_experts):
    per = n_experts // N_GROUPS
    logit = [lg_ref[e] for e in range(n_experts)]
    mx = functools.reduce(jnp.maximum, logit)
    ex = [jnp.exp(l - mx) for l in logit]
    tot = functools.reduce(lambda a, c: a + c, ex)
    p = [e / tot for e in ex]

    def top2_sum(vals):
        a, bb, c, d = vals
        hi1, lo1 = jnp.maximum(a, bb), jnp.minimum(a, bb)
        hi2, lo2 = jnp.maximum(c, d), jnp.minimum(c, d)
        first = jnp.maximum(hi1, hi2)
        second = jnp.maximum(jnp.minimum(hi1, hi2), jnp.maximum(lo1, lo2))
        return first + second

    assert per == 4
    gs = [top2_sum(p[g * per:(g + 1) * per]) for g in range(N_GROUPS)]
    best = jnp.zeros_like(gs[0], dtype=jnp.int32)
    best_v = gs[0]
    for g in range(1, N_GROUPS):
        upd = gs[g] > best_v
        best = jnp.where(upd, g, best)
        best_v = jnp.where(upd, gs[g], best_v)
    vals = []
    for k in range(per):
        v = p[k]
        for g in range(1, N_GROUPS):
            v = jnp.where(best == g, p[g * per + k], v)
        vals.append(v)
    i1 = jnp.zeros_like(best)
    v1 = vals[0]
    for k in range(1, per):
        upd = vals[k] > v1
        i1 = jnp.where(upd, k, i1)
        v1 = jnp.where(upd, vals[k], v1)
    i2 = jnp.full_like(best, -1)
    v2 = jnp.full_like(v1, -1.0)
    for k in range(per):
        upd = (i1 != k) & (vals[k] > v2)
        i2 = jnp.where(upd, k, i2)
        v2 = jnp.where(upd, vals[k], v2)
    den = v1 + v2
    picks = (best * per + i1, best * per + i2)
    idx_ref[0] = picks[0]
    idx_ref[1] = picks[1]
    w_ref[0] = v1 / den
    w_ref[1] = v2 / den

    rows, lanes = best.shape
    li = lax.broadcasted_iota(jnp.int32, (lanes, lanes), 0)
    lj = lax.broadcasted_iota(jnp.int32, (lanes, lanes), 1)
    upper = jnp.where(li <= lj, 1.0, 0.0).astype(BF16)
    ones = jnp.ones((lanes, lanes), BF16)
    ri = lax.broadcasted_iota(jnp.int32, (rows, rows), 0)
    rj = lax.broadcasted_iota(jnp.int32, (rows, rows), 1)
    before = jnp.where(rj < ri, 1.0, 0.0).astype(BF16)
    ranks = [jnp.zeros((rows, lanes), F32), jnp.zeros((rows, lanes), F32)]
    for e in range(n_experts):
        base = jnp.zeros((1, lanes), F32)
        for s in range(2):
            hit = picks[s] == e
            ind = jnp.where(hit, 1.0, 0.0).astype(BF16)
            in_row = _dot(ind, upper)
            row_tot = _dot(ind, ones)
            row_off = _dot(before, row_tot.astype(BF16))
            ranks[s] = jnp.where(hit, in_row + row_off + (base - 1.0), ranks[s])
            base = base + (row_off + row_tot)[rows - 1:rows, :]
        cnt_ref[pl.ds(e, 1), :] = base.astype(jnp.int32)
    rank_ref[0] = ranks[0].astype(jnp.int32)
    rank_ref[1] = ranks[1].astype(jnp.int32)


def _route(logits_t):
    n_experts, rows, lanes = logits_t.shape
    return pl.pallas_call(
        functools.partial(_route_kernel, n_experts=n_experts),
        out_shape=[jax.ShapeDtypeStruct((2, rows, lanes), jnp.int32),
                   jax.ShapeDtypeStruct((2, rows, lanes), F32),
                   jax.ShapeDtypeStruct((2, rows, lanes), jnp.int32),
                   jax.ShapeDtypeStruct((n_experts, lanes), jnp.int32)],
        name="route",
    )(logits_t)


def _dispatch_kernel(pos_ref, x_ref, init_hbm, o_hbm, sem, *, m):
    del init_hbm
    i = pl.program_id(0)
    chunk = x_ref.shape[0]

    def row_copy(r, dst):
        return pltpu.make_async_copy(x_ref.at[pl.ds(r, 1), :], o_hbm.at[pl.ds(dst, 1), :], sem)

    def issue(r, carry):
        row_copy(r, pos_ref[i * chunk + r]).start()
        row_copy(r, pos_ref[m + i * chunk + r]).start()
        return carry
    lax.fori_loop(0, chunk, issue, 0, unroll=8)

    def drain(r, carry):
        row_copy(r, 0).wait()
        row_copy(r, 0).wait()
        return carry
    lax.fori_loop(0, chunk, drain, 0, unroll=8)


def _dispatch_rows(xp, pos, npad):
    m, n = xp.shape
    chunk = _pick(m, 512, 8)
    return pl.pallas_call(
        functools.partial(_dispatch_kernel, m=m),
        grid_spec=pltpu.PrefetchScalarGridSpec(
            num_scalar_prefetch=1, grid=(m // chunk,),
            in_specs=[pl.BlockSpec((chunk, n), lambda i, pos: (i, 0)), pl.BlockSpec(memory_space=pl.ANY)],
            out_specs=pl.BlockSpec(memory_space=pl.ANY),
            scratch_shapes=[pltpu.SemaphoreType.DMA(())]),
        out_shape=jax.ShapeDtypeStruct((npad, n), xp.dtype),
        input_output_aliases={2: 0},
        name="expert_dispatch",
        compiler_params=pltpu.CompilerParams(dimension_semantics=("arbitrary",)),
    )(pos, xp, jnp.zeros((npad, n), xp.dtype))


def _new_expert(te_ref, t):
    return (t == 0) | (te_ref[t] != te_ref[jnp.maximum(t - 1, 0)])


def _expert_up_kernel(te_ref, tr_ref, nt_ref, x_ref, wg_ref, wu_ref, o_ref, wg_s, wu_s):
    t = pl.program_id(1)
    live = t < nt_ref[0]
    n_valid = tr_ref[t]
    tm, half_d = x_ref.shape
    half_m = tm // 2

    @pl.when(live & _new_expert(te_ref, t))
    def _():
        wg_s[...] = wg_ref[...].astype(BF16)
        wu_s[...] = wu_ref[...].astype(BF16)

    def compute(nr):
        x_hi, x_lo = _unpack_halves(x_ref[pl.ds(0, nr), :])
        x_hi, x_lo = x_hi.astype(BF16), x_lo.astype(BF16)
        top, bot = pl.ds(0, half_d), pl.ds(half_d, half_d)
        g = _dot(x_hi, wg_s[top, :]) + _dot(x_lo, wg_s[bot, :])
        u = _dot(x_hi, wu_s[top, :]) + _dot(x_lo, wu_s[bot, :])
        o_ref[pl.ds(0, nr), :] = (g * jax.nn.sigmoid(g) * u).astype(o_ref.dtype)

    @pl.when(live & (n_valid > half_m))
    def _():
        compute(tm)

    @pl.when(live & (n_valid <= half_m))
    def _():
        compute(half_m)
        o_ref[pl.ds(half_m, half_m), :] = jnp.zeros((half_m, o_ref.shape[1]), o_ref.dtype)

    @pl.when(jnp.logical_not(live))
    def _():
        o_ref[...] = jnp.zeros_like(o_ref)


def _expert_up(xg, w_gate, w_up, layer, tile_e, tile_rows, n_tiles, *, tm):
    npad, half_d = xg.shape
    d = 2 * half_d
    f = w_gate.shape[-1]
    tf = _pick(f, 512)
    w_spec = pl.BlockSpec((None, None, d, tf), lambda j, t, te, tr, nt: (layer, te[t], 0, j))
    return pl.pallas_call(
        _expert_up_kernel,
        grid_spec=pltpu.PrefetchScalarGridSpec(
            num_scalar_prefetch=3, grid=(f // tf, npad // tm),
            in_specs=[pl.BlockSpec((tm, half_d), lambda j, t, te, tr, nt: (t, 0)), w_spec, w_spec],
            out_specs=pl.BlockSpec((tm, tf), lambda j, t, te, tr, nt: (t, j)),
            scratch_shapes=[pltpu.VMEM((d, tf), BF16), pltpu.VMEM((d, tf), BF16)]),
        out_shape=jax.ShapeDtypeStruct((npad, f), BF16),
        name="expert_up",
        compiler_params=pltpu.CompilerParams(
            dimension_semantics=("arbitrary", "arbitrary"),
            vmem_limit_bytes=_vmem_limit(4 * d * tf * 4 + 2 * d * tf * 2 + 5 * tm * d * 2 + 8 * tm * tf * 4)),
    )(tile_e, tile_rows, n_tiles, xg, w_gate, w_up)


def _expert_down_kernel(te_ref, tr_ref, nt_ref, h_ref, wd_ref, o_ref, wd_s):
    t = pl.program_id(1)
    live = t < nt_ref[0]
    n_valid = tr_ref[t]
    tm = h_ref.shape[0]
    half_m = tm // 2

    @pl.when(live & _new_expert(te_ref, t))
    def _():
        wd_s[...] = wd_ref[...].astype(BF16)

    def compute(nr):
        o_ref[pl.ds(0, nr), :] = _pack_halves(_dot(h_ref[pl.ds(0, nr), :], wd_s[...]))

    @pl.when(live & (n_valid > half_m))
    def _():
        compute(tm)

    @pl.when(live & (n_valid <= half_m))
    def _():
        compute(half_m)
        o_ref[pl.ds(half_m, half_m), :] = jnp.zeros((half_m, o_ref.shape[1]), o_ref.dtype)

    @pl.when(jnp.logical_not(live))
    def _():
        o_ref[...] = jnp.zeros_like(o_ref)


def _expert_down(hg, w_down, layer, tile_e, tile_rows, n_tiles, *, tm, tn):
    npad, f = hg.shape
    d = w_down.shape[-1]
    return pl.pallas_call(
        _expert_down_kernel,
        grid_spec=pltpu.PrefetchScalarGridSpec(
            num_scalar_prefetch=3, grid=(d // tn, npad // tm),
            in_specs=[pl.BlockSpec((tm, f), lambda j, t, te, tr, nt: (t, 0)),
                      pl.BlockSpec((None, None, f, tn), lambda j, t, te, tr, nt: (layer, te[t], 0, j))],
            out_specs=pl.BlockSpec((tm, tn // 2), lambda j, t, te, tr, nt: (t, j)),
            scratch_shapes=[pltpu.VMEM((f, tn), BF16)]),
        out_shape=jax.ShapeDtypeStruct((npad, d // 2), jnp.uint32),
        name="expert_down",
        compiler_params=pltpu.CompilerParams(
            dimension_semantics=("arbitrary", "arbitrary"),
            vmem_limit_bytes=_vmem_limit(2 * f * tn * 4 + f * tn * 2 + 2 * tm * f * 2 + 6 * tm * tn * 4)),
    )(tile_e, tile_rows, n_tiles, hg, w_down)


def _combine_ln_kernel(pos_ref, y_hbm, x_ref, w0_ref, w1_ref, g_ref, b_ref, o_ref, ob_ref, buf, sem,
                       *, alpha, m, tn):
    i = pl.program_id(0)
    tc, d = x_ref.shape

    def row_copy(s, r, p):
        return pltpu.make_async_copy(y_hbm.at[pl.ds(p, 1), :], buf.at[s, pl.ds(r, 1), :], sem)

    def issue(r, carry):
        row_copy(0, r, pos_ref[i * tc + r]).start()
        row_copy(1, r, pos_ref[m + i * tc + r]).start()
        return carry
    lax.fori_loop(0, tc, issue, 0, unroll=8)

    def drain(r, carry):
        row_copy(0, r, 0).wait()
        row_copy(1, r, 0).wait()
        return carry
    lax.fori_loop(0, tc, drain, 0, unroll=8)

    w0, w1 = w0_ref[...], w1_ref[...]
    parts = []
    for j in range(d // tn):
        cs = pl.ds(j * (tn // 2), tn // 2)
        hi0, lo0 = _unpack_halves(buf[0, :, cs])
        hi1, lo1 = _unpack_halves(buf[1, :, cs])
        parts += [w0 * hi0 + w1 * hi1, w0 * lo0 + w1 * lo1]
    z = alpha * x_ref[...] + jnp.concatenate(parts, axis=1)
    mu = jnp.mean(z, axis=-1, keepdims=True)
    dz = z - mu
    var = jnp.mean(dz * dz, axis=-1, keepdims=True)
    x = dz * lax.rsqrt(var + LN_EPS) * g_ref[...] + b_ref[...]
    o_ref[...] = x
    ob_ref[...] = x.astype(BF16)


def _combine_ln(yg, pos, ewts, x, ln_g, ln_b, idx, *, alpha, tn):
    m, d = x.shape
    tc = _pick(m, 256, 8)
    row = lambda i, pos: (i, 0)
    par = lambda i, pos: (idx, 0, 0)
    return pl.pallas_call(
        functools.partial(_combine_ln_kernel, alpha=alpha, m=m, tn=tn),
        grid_spec=pltpu.PrefetchScalarGridSpec(
            num_scalar_prefetch=1, grid=(m // tc,),
            in_specs=[pl.BlockSpec(memory_space=pl.ANY),
                      pl.BlockSpec((tc, d), row),
                      pl.BlockSpec((None, tc, 1), lambda i, pos: (0, i, 0)),
                      pl.BlockSpec((None, tc, 1), lambda i, pos: (1, i, 0)),
                      pl.BlockSpec((None, 1, d), par), pl.BlockSpec((None, 1, d), par)],
            out_specs=[pl.BlockSpec((tc, d), row), pl.BlockSpec((tc, d), row)],
            scratch_shapes=[pltpu.VMEM((2, tc, d // 2), jnp.uint32), pltpu.SemaphoreType.DMA(())]),
        out_shape=[jax.ShapeDtypeStruct((m, d), F32), jax.ShapeDtypeStruct((m, d), BF16)],
        name="combine_ln",
        compiler_params=pltpu.CompilerParams(
            dimension_semantics=("arbitrary",), vmem_limit_bytes=_vmem_limit(tc * d * 48)),
    )(pos, yg, x, ewts, ewts, ln_g, ln_b)


def _expert_schedule(eidx, rank, counts, tm, n_tiles_max):
    n_experts = counts.shape[0]
    tiles = (counts + tm - 1) // tm
    tile_end = jnp.cumsum(tiles)
    tile_start = tile_end - tiles
    experts = jnp.arange(n_experts, dtype=jnp.int32)
    pick = eidx[:, :, None] == experts
    pos = rank + jnp.sum(jnp.where(pick, tile_start * tm, 0), axis=-1)
    n_tiles = tile_end[-1]
    tile_ids = jnp.arange(n_tiles_max, dtype=jnp.int32)
    tile_e = jnp.sum(tile_end[None, :] <= jnp.minimum(tile_ids, n_tiles - 1)[:, None], axis=-1)
    mine = tile_e[:, None] == experts
    filled = jnp.sum(jnp.where(mine, counts - (tile_ids[:, None] - tile_start) * tm, 0), axis=-1)
    tile_rows = jnp.clip(filled, 0, tm)
    i32 = lambda a: a.astype(jnp.int32)
    return i32(pos).reshape(-1), i32(tile_e), i32(tile_rows), i32(n_tiles).reshape(1)


def kernel(x, p, w_in, b_branch_gate, w_gla_gate_up, b_gla_gate, conv_w, conv_b, conv_ln_g, conv_ln_b, w_conv_out, gla_norm_g, w_gla_out, w_out, ln_g, ln_b, w_router, b_router, w_exp_gate, w_exp_up, w_exp_down, w_ple, w_ple_gate):
    bsz, seq, d = x.shape
    assert bsz == 1
    depth = w_in.shape[0]
    m = bsz * seq
    conv_ch = conv_w.shape[-1]
    rank, dk = w_gla_gate_up.shape[1], w_gla_gate_up.shape[2]
    dv = gla_norm_g.shape[-1]
    n_experts = w_router.shape[1]
    off_b = conv_ch
    off_q = 2 * conv_ch
    off_alpha = off_q + 2 * dk + 2 * dv
    off_gate = off_alpha + rank
    alpha = (2.0 * depth) ** 0.25
    tm_e = 512 if m >= 4096 else 64
    n_tiles_max = (2 * m) // tm_e + n_experts
    tn_down = _pick(d, 2048)

    xf = x.reshape(m, d)
    xb = xf.astype(BF16)
    pb = p.reshape(depth, m, p.shape[-1]).astype(BF16)
    w_in_t = jnp.swapaxes(w_in, 1, 2)
    b_gate = b_branch_gate.reshape(depth, 1, 2 * d)
    wup_pad = jnp.zeros((depth, LANES, dk), F32).at[:, :rank, :].set(w_gla_gate_up)
    b_a = b_gla_gate.reshape(depth, 1, dk)
    conv_b3 = conv_b.reshape(depth, 1, conv_ch)
    cln_g = conv_ln_g.reshape(depth, 1, conv_ch)
    cln_b = conv_ln_b.reshape(depth, 1, conv_ch)
    gnorm = gla_norm_g.reshape(depth, 1, dv)
    ln_g3 = ln_g.reshape(depth * 3, 1, d)
    ln_b3 = ln_b.reshape(depth * 3, 1, d)
    wr_pad = jnp.zeros((d, LANES), F32).at[:, :n_experts].set(w_router)
    br_pad = jnp.zeros((1, LANES), F32).at[0, :n_experts].set(b_router)

    tm = _pick(m, 1024, 8)
    for i in range(depth):
        (u,) = _fused_matmul(
            [xb], [(0, w_in_t, i, 0, True), (0, w_in_t, i, off_b, True)], [],
            lambda ds, ex: [ds[0] * jax.nn.sigmoid(ds[1])], [F32],
            n_cols=conv_ch, tm=tm, tn=_pick(conv_ch, 256), name="in_proj_glu")
        (h2,) = _fused_matmul(
            [xb], [(0, w_in_t, i, off_q, True)], [], lambda ds, ex: [ds[0]], [F32],
            n_cols=2 * dk + 2 * dv, tm=tm, tn=_pick(dk, 512), name="in_proj_qkvr")
        (a_low,) = _fused_matmul(
            [xb], [(0, w_in_t, i, off_alpha, True)], [], lambda ds, ex: [ds[0]], [F32],
            n_cols=LANES, tm=tm, tn=LANES, name="in_proj_alpha")
        (gates,) = _fused_matmul(
            [xb], [(0, w_in_t, i, off_gate, True)], [('row', b_gate, i, 0)],
            lambda ds, ex: [jax.nn.sigmoid(ds[0] + ex[0])], [BF16],
            n_cols=2 * d, tm=tm, tn=_pick(d, 512), name="in_proj_gates")

        uc = _conv_ln_silu(u, conv_w, conv_b3, cln_g, cln_b, i)
        og = _gla(h2, a_low, wup_pad, b_a, gnorm, i, dk=dk, dv=dv, rank=rank)

        tn_o = _pick(d, 512)
        (merged,) = _fused_matmul(
            [uc, og], [(0, w_conv_out, i, 0, False), (1, w_gla_out, i, 0, False)],
            [('tile', gates, None, 0), ('tile', gates, None, d)],
            lambda ds, ex: [ex[0].astype(F32) * ds[0] + ex[1].astype(F32) * ds[1]], [BF16],
            n_cols=d, tm=tm, tn=tn_o, name="branch_out_merge")
        (z1,) = _fused_matmul(
            [merged], [(0, w_out, i, 0, False)], [('tile', xf, None, 0)],
            lambda ds, ex: [alpha * ex[0] + ds[0]], [F32],
            n_cols=d, tm=tm, tn=tn_o, name="mixer_out")
        x1, x1p, logits = _layer_norm(z1, ln_g3, ln_b3, 3 * i, router=(wr_pad, br_pad))

        logits_t = logits[:, :n_experts].T.reshape(n_experts, m // LANES, LANES)
        eidx, ewts, pair_rank, counts = _route(logits_t)
        pos, tile_e, tile_rows, n_tiles = _expert_schedule(
            eidx.reshape(2, m), pair_rank.reshape(2, m), counts[:, 0], tm_e, n_tiles_max)
        xg = _dispatch_rows(x1p, pos, n_tiles_max * tm_e)
        hg = _expert_up(xg, w_exp_gate, w_exp_up, i, tile_e, tile_rows, n_tiles, tm=tm_e)
        yg = _expert_down(hg, w_exp_down, i, tile_e, tile_rows, n_tiles, tm=tm_e, tn=tn_down)
        x2, x2b = _combine_ln(yg, pos, ewts.reshape(2, m, 1), x1, ln_g3, ln_b3, 3 * i + 1,
                              alpha=alpha, tn=tn_down)

        (z3,) = _fused_matmul(
            [x2b, pb[i]], [(0, w_ple_gate, i, 0, False), (1, w_ple, i, 0, False)], [('tile', x2, None, 0)],
            lambda ds, ex: [alpha * ex[0] + jax.nn.sigmoid(ds[0]) * ds[1]], [F32],
            n_cols=d, tm=tm, tn=tn_o, name="ple")
        xf, xb = _layer_norm(z3, ln_g3, ln_b3, 3 * i + 2)

    return xf.reshape(bsz, seq, d)
```

```python
import functools

import jax
import jax.numpy as jnp
from jax import lax
from jax.experimental import pallas as pl
from jax.experimental.pallas import tpu as pltpu

F32 = jnp.float32
BF16 = jnp.bfloat16

GLA_HEADS = 4
GLA_GATE_TEMP = 16.0
GLA_CHUNK = 64
N_GROUPS = 4
LN_EPS = 1e-5
LANES = 128
V7X_SCOPED_VMEM_BYTES = 60000 * 1024


def _pick(n, pref, mult=LANES):
    if n <= pref:
        return n
    t = (pref // mult) * mult
    while t >= mult:
        if n % t == 0:
            return t
        t -= mult
    return n


def _vmem_limit(nbytes):
    return int(min(max(nbytes * 5 // 4 + (4 << 20), 16 << 20), V7X_SCOPED_VMEM_BYTES))


def _split3(x):
    hi = x.astype(BF16)
    r1 = x - hi.astype(F32)
    mid = r1.astype(BF16)
    lo = (r1 - mid.astype(F32)).astype(BF16)
    return hi, mid, lo


def _dot(a, b):
    return jnp.dot(a, b, preferred_element_type=F32)


def _dot_f32(a, b):
    ah, am, _ = _split3(a)
    bh, bm, _ = _split3(b)
    return _dot(ah, bh) + (_dot(ah, bm) + _dot(am, bh))


NT_DIMS = (((1,), (1,)), ((), ()))


def _fused_matmul_kernel(*refs, n_acts, dot_act, forms, n_extras, n_outs, epilogue):
    n_w = sum(2 if shift else 1 for _, shift in forms)
    act_refs = refs[:n_acts]
    w_refs = list(refs[n_acts:n_acts + n_w])
    ex_refs = refs[n_acts + n_w:n_acts + n_w + n_extras]
    out_refs = refs[n_acts + n_w + n_extras:n_acts + n_w + n_extras + n_outs]
    w_bf16 = refs[n_acts + n_w + n_extras + n_outs:]

    @pl.when(pl.program_id(1) == 0)
    def _():
        for s, (_, shift) in zip(w_bf16, forms):
            w_ref = w_refs.pop(0)
            if shift:
                tn = s.shape[0]
                s[pl.ds(0, tn - shift), :] = w_ref[pl.ds(shift, tn - shift), :].astype(BF16)
                s[pl.ds(tn - shift, shift), :] = w_refs.pop(0)[...].astype(BF16)
            else:
                s[...] = w_ref[...].astype(BF16)

    dots = []
    for ai, s, (output_major, _) in zip(dot_act, w_bf16, forms):
        a = act_refs[ai][...]
        dots.append(lax.dot_general(a, s[...], NT_DIMS, preferred_element_type=F32) if output_major
                    else _dot(a, s[...]))
    outs = epilogue(dots, [e[...] for e in ex_refs])
    for o_ref, o in zip(out_refs, outs):
        o_ref[...] = o.astype(o_ref.dtype)


def _fused_matmul(acts, dots, extras, epilogue, out_dtypes, *, n_cols, tm, tn, name):
    m = acts[0].shape[0]
    assert m % tm == 0 and n_cols % tn == 0
    grid = (n_cols // tn, m // tm)
    in_specs, args, scratch, forms = [], [], [], []
    nbytes = 0
    for a in acts:
        k = a.shape[1]
        in_specs.append(pl.BlockSpec((tm, k), lambda j, i: (i, 0)))
        args.append(a)
        nbytes += 2 * tm * k * a.dtype.itemsize
    for (_, w, layer, off, output_major) in dots:
        shift = off % tn
        if output_major:
            k = w.shape[2]
            in_specs.append(pl.BlockSpec((None, tn, k), lambda j, i, layer=layer, cb=off // tn: (layer, cb + j, 0)))
            args.append(w)
            if shift:
                assert shift % 16 == 0 and tn % shift == 0 and (off - shift) % shift == 0
                in_specs.append(pl.BlockSpec(
                    (None, shift, k),
                    lambda j, i, layer=layer, cb=off // tn, r=tn // shift: (layer, (cb + j + 1) * r, 0)))
                args.append(w)
                nbytes += 2 * shift * k * 4
            scratch.append(pltpu.VMEM((tn, k), BF16))
        else:
            assert shift == 0
            k = w.shape[1]
            in_specs.append(pl.BlockSpec((None, k, tn), lambda j, i, layer=layer, cb=off // tn: (layer, 0, cb + j)))
            args.append(w)
            scratch.append(pltpu.VMEM((k, tn), BF16))
        forms.append((output_major, shift))
        nbytes += 2 * k * tn * 4 + k * tn * 2
    for (kind, arr, layer, off) in extras:
        assert off % tn == 0
        if kind == 'tile':
            in_specs.append(pl.BlockSpec((tm, tn), lambda j, i, cb=off // tn: (i, cb + j)))
            nbytes += 2 * tm * tn * arr.dtype.itemsize
        else:
            in_specs.append(pl.BlockSpec((None, 1, tn), lambda j, i, layer=layer, cb=off // tn: (layer, 0, cb + j)))
        args.append(arr)
    out_specs = [pl.BlockSpec((tm, tn), lambda j, i: (i, j)) for _ in out_dtypes]
    out_shape = [jax.ShapeDtypeStruct((m, n_cols), dt) for dt in out_dtypes]
    nbytes += sum(2 * tm * tn * jnp.dtype(dt).itemsize for dt in out_dtypes)
    nbytes += (len(dots) + 3) * tm * tn * 4
    kernel = functools.partial(
        _fused_matmul_kernel, n_acts=len(acts), dot_act=tuple(d[0] for d in dots), forms=tuple(forms),
        n_extras=len(extras), n_outs=len(out_dtypes), epilogue=epilogue)
    return pl.pallas_call(
        kernel, grid=grid, in_specs=in_specs, out_specs=out_specs, out_shape=out_shape,
        scratch_shapes=scratch, name=name,
        compiler_params=pltpu.CompilerParams(
            dimension_semantics=("arbitrary", "arbitrary"), vmem_limit_bytes=_vmem_limit(nbytes)),
    )(*args)


def _conv_kernel(u_ref, prev_ref, w_ref, b_ref, g_ref, beta_ref, o_ref, win, shifted, ybuf, *, kw, halo, rows):
    i = pl.program_id(0)
    j = pl.program_id(1)
    tt, cc = u_ref.shape
    win[pl.ds(0, halo), :] = jnp.where(i > 0, prev_ref[...], 0.0)
    win[pl.ds(halo, tt), :] = u_ref[...]
    span = shifted.shape[1]
    for s in range(1, 8):
        shifted[s - 1] = win[pl.ds(s, span), :]
    lead = halo - (kw - 1)
    for r0 in range(0, tt, rows):
        acc = jnp.broadcast_to(b_ref[...], (rows, cc))
        for t in range(kw):
            s = (lead + t) % 8
            start = r0 + lead + t - s
            src = win[pl.ds(start, rows), :] if s == 0 else shifted[s - 1, pl.ds(start, rows), :]
            acc = acc + src * w_ref[pl.ds(t, 1), :]
        ybuf[j, pl.ds(r0, rows), :] = acc

    @pl.when(j == pl.num_programs(1) - 1)
    def _():
        nc = ybuf.shape[0]
        c = nc * cc
        s = jnp.zeros((tt, 1), F32)
        for k in range(nc):
            s = s + jnp.sum(ybuf[k], axis=-1, keepdims=True)
        mu = s * (1.0 / c)
        v = jnp.zeros((tt, 1), F32)
        for k in range(nc):
            d = ybuf[k] - mu
            v = v + jnp.sum(d * d, axis=-1, keepdims=True)
        rstd = lax.rsqrt(v * (1.0 / c) + LN_EPS)
        for k in range(nc):
            cs = slice(k * cc, (k + 1) * cc)
            y = (ybuf[k] - mu) * rstd * g_ref[:, cs] + beta_ref[:, cs]
            o_ref[:, cs] = (y * jax.nn.sigmoid(y)).astype(o_ref.dtype)


def _conv_ln_silu(u, conv_w, conv_b, ln_g, ln_b, layer):
    m, c = u.shape
    kw = conv_w.shape[1]
    halo = -(-(kw - 1) // 8) * 8
    tt = _pick(m, 512, halo)
    cc = _pick(c, 256)
    rows = 64 if tt % 64 == 0 else tt
    kernel = functools.partial(_conv_kernel, kw=kw, halo=halo, rows=rows)
    return pl.pallas_call(
        kernel, grid=(m // tt, c // cc),
        in_specs=[
            pl.BlockSpec((tt, cc), lambda i, j: (i, j)),
            pl.BlockSpec((halo, cc), lambda i, j: (jnp.maximum(i * (tt // halo) - 1, 0), j)),
            pl.BlockSpec((None, kw, cc), lambda i, j: (layer, 0, j)),
            pl.BlockSpec((None, 1, cc), lambda i, j: (layer, 0, j)),
            pl.BlockSpec((None, 1, c), lambda i, j: (layer, 0, 0)),
            pl.BlockSpec((None, 1, c), lambda i, j: (layer, 0, 0)),
        ],
        out_specs=pl.BlockSpec((tt, c), lambda i, j: (i, 0)),
        out_shape=jax.ShapeDtypeStruct((m, c), BF16),
        scratch_shapes=[pltpu.VMEM((tt + halo, cc), F32), pltpu.VMEM((7, tt + halo - 8, cc), F32),
                        pltpu.VMEM((c // cc, tt, cc), F32)],
        name="conv_ln_silu",
        compiler_params=pltpu.CompilerParams(
            dimension_semantics=("arbitrary", "arbitrary"),
            vmem_limit_bytes=_vmem_limit(3 * tt * c * 4 + 14 * (tt + halo) * cc * 4)),
    )(u, u, conv_w, conv_b, ln_g, ln_b)


def _gla_kernel(q_ref, k_ref, v_ref, r_ref, a_ref, wup_ref, ba_ref, g_ref, o_ref,
                state, qe_s, qi_s, ki_s, kd_s, dec_s, obuf, *, rank, chunk):
    tb, dk = q_ref.shape
    dv = v_ref.shape[1]
    hk, hv = dk // GLA_HEADS, dv // GLA_HEADS
    nchunks = tb // chunk

    @pl.when(pl.program_id(0) == 0)
    def _():
        state[...] = jnp.zeros_like(state)

    lane = lax.broadcasted_iota(jnp.int32, a_ref.shape, 1)
    a_low = jnp.where(lane < rank, a_ref[...], 0.0)
    z = _dot_f32(a_low, wup_ref[...]) + ba_ref[...]
    la = (jnp.minimum(z, 0.0) - jnp.log1p(jnp.exp(-jnp.abs(z)))) * (1.0 / GLA_GATE_TEMP)
    la3 = _split3(la)

    rr = lax.broadcasted_iota(jnp.int32, (tb, tb), 0)
    cc = lax.broadcasted_iota(jnp.int32, (tb, tb), 1)
    sh = chunk.bit_length() - 1
    assert chunk == 1 << sh
    same = (rr >> sh) == (cc >> sh)
    tril = jnp.where(same & (cc <= rr), 1.0, 0.0).astype(BF16)
    ones_blk = jnp.where(same, 1.0, 0.0).astype(BF16)
    b = _dot(tril, la3[0]) + (_dot(tril, la3[1]) + _dot(tril, la3[2]))
    bl = _dot(ones_blk, la3[0]) + (_dot(ones_blk, la3[1]) + _dot(ones_blk, la3[2]))

    sr = lax.broadcasted_iota(jnp.int32, (tb, nchunks * LANES), 0)
    sc = lax.broadcasted_iota(jnp.int32, (tb, nchunks * LANES), 1)
    sel = jnp.where((sr >> sh) == (sc >> (LANES.bit_length() - 1)), 1.0, 0.0).astype(BF16)
    tn_dims = (((0,), (0,)), ((), ()))
    tot_t = (lax.dot_general(la3[0], sel, tn_dims, preferred_element_type=F32)
             + (lax.dot_general(la3[1], sel, tn_dims, preferred_element_type=F32)
                + lax.dot_general(la3[2], sel, tn_dims, preferred_element_type=F32)))
    dec_s[...] = jnp.exp(tot_t)

    q = q_ref[...].astype(F32) * (float(hk) ** -0.5)
    k = k_ref[...].astype(F32)
    half = 0.5 * bl
    qe_s[...] = (q * jnp.exp(b)).astype(BF16)
    qi_s[...] = (q * jnp.exp(b - half)).astype(BF16)
    ki_s[...] = (k * jnp.exp(half - b)).astype(BF16)
    kd_s[...] = (k * jnp.exp(bl - b)).astype(BF16)

    ri = lax.broadcasted_iota(jnp.int32, (chunk, chunk), 0)
    ci = lax.broadcasted_iota(jnp.int32, (chunk, chunk), 1)
    causal = ci <= ri
    nt_dims = (((1,), (1,)), ((), ()))
    for c in range(nchunks):
        rows = pl.ds(c * chunk, chunk)
        for h in range(GLA_HEADS):
            ks = pl.ds(h * hk, hk)
            vs = pl.ds(h * hv, hv)
            vch = v_ref[rows, vs].astype(BF16)
            scores = lax.dot_general(qi_s[rows, ks], ki_s[rows, ks], nt_dims, preferred_element_type=F32)
            scores = jnp.where(causal, scores, 0.0).astype(BF16)
            s_old = state[h]
            o = _dot(qe_s[rows, ks], s_old.astype(BF16)) + _dot(scores, vch)
            obuf[rows, vs] = o
            dcol = dec_s[ks, pl.ds(c * LANES, LANES)]
            decay = jnp.concatenate([dcol] * (hv // LANES), axis=1) if hv >= LANES else dcol[:, :hv]
            state[h] = s_old * decay + lax.dot_general(kd_s[rows, ks], vch, tn_dims, preferred_element_type=F32)

    for h in range(GLA_HEADS):
        vs = pl.ds(h * hv, hv)
        o = obuf[:, vs]
        ms = jnp.mean(o * o, axis=-1, keepdims=True)
        on = o * lax.rsqrt(ms + LN_EPS) * g_ref[:, vs]
        r = r_ref[:, vs].astype(F32)
        o_ref[:, vs] = (on * (r * jax.nn.sigmoid(r))).astype(o_ref.dtype)


def _gla(h2, a_low, wup_pad, b_a, norm_g, layer, *, dk, dv, rank):
    m = h2.shape[0]
    tb = _pick(m, 256, GLA_CHUNK)
    assert tb % GLA_CHUNK == 0 and dv == 2 * dk
    nchunks = tb // GLA_CHUNK
    hk, hv = dk // GLA_HEADS, dv // GLA_HEADS
    kernel = functools.partial(_gla_kernel, rank=rank, chunk=GLA_CHUNK)
    return pl.pallas_call(
        kernel, grid=(m // tb,),
        in_specs=[
            pl.BlockSpec((tb, dk), lambda i: (i, 0)),
            pl.BlockSpec((tb, dk), lambda i: (i, 1)),
            pl.BlockSpec((tb, dv), lambda i: (i, 1)),
            pl.BlockSpec((tb, dv), lambda i: (i, 2)),
            pl.BlockSpec((tb, LANES), lambda i: (i, 0)),
            pl.BlockSpec((None, LANES, dk), lambda i: (layer, 0, 0)),
            pl.BlockSpec((None, 1, dk), lambda i: (layer, 0, 0)),
            pl.BlockSpec((None, 1, dv), lambda i: (layer, 0, 0)),
        ],
        out_specs=pl.BlockSpec((tb, dv), lambda i: (i, 0)),
        out_shape=jax.ShapeDtypeStruct((m, dv), BF16),
        scratch_shapes=[
            pltpu.VMEM((GLA_HEADS, hk, hv), F32),
            pltpu.VMEM((tb, dk), BF16), pltpu.VMEM((tb, dk), BF16),
            pltpu.VMEM((tb, dk), BF16), pltpu.VMEM((tb, dk), BF16),
            pltpu.VMEM((dk, nchunks * LANES), F32),
            pltpu.VMEM((tb, dv), F32),
        ],
        name="gla",
        compiler_params=pltpu.CompilerParams(
            dimension_semantics=("arbitrary",), vmem_limit_bytes=_vmem_limit(48 << 20)),
    )(h2, h2, h2, h2, a_low, wup_pad, b_a, norm_g)


def _pack_halves(x):
    n = x.shape[1] // 2
    hi = lax.bitcast_convert_type(x[:, :n].astype(BF16).astype(F32), jnp.uint32)
    lo = lax.bitcast_convert_type(x[:, n:].astype(BF16).astype(F32), jnp.uint32)
    return hi | (lo >> 16)


def _unpack_halves(w):
    hi = lax.bitcast_convert_type(w & jnp.uint32(0xFFFF0000), F32)
    lo = lax.bitcast_convert_type(w << 16, F32)
    return hi, lo


def _ln_kernel(z_ref, g_ref, b_ref, *rest, with_router):
    if with_router:
        wr_ref, br_ref, x_ref, xp_ref, lg_ref = rest
    else:
        x_ref, xb_ref = rest
    z = z_ref[...]
    mu = jnp.mean(z, axis=-1, keepdims=True)
    d = z - mu
    var = jnp.mean(d * d, axis=-1, keepdims=True)
    x = d * lax.rsqrt(var + LN_EPS) * g_ref[...] + b_ref[...]
    x_ref[...] = x
    if with_router:
        xp_ref[...] = _pack_halves(x)
        lg_ref[...] = _dot_f32(x, wr_ref[...]) + br_ref[...]
    else:
        xb_ref[...] = x.astype(BF16)


def _layer_norm(z, ln_g, ln_b, idx, router=None):
    m, d = z.shape
    tr = _pick(m, 512, 8)
    in_specs = [
        pl.BlockSpec((tr, d), lambda i: (i, 0)),
        pl.BlockSpec((None, 1, d), lambda i: (idx, 0, 0)),
        pl.BlockSpec((None, 1, d), lambda i: (idx, 0, 0)),
    ]
    args = [z, ln_g, ln_b]
    if router is None:
        out_specs = [pl.BlockSpec((tr, d), lambda i: (i, 0)), pl.BlockSpec((tr, d), lambda i: (i, 0))]
        out_shape = [jax.ShapeDtypeStruct((m, d), F32), jax.ShapeDtypeStruct((m, d), BF16)]
    else:
        in_specs += [pl.BlockSpec((d, LANES), lambda i: (0, 0)), pl.BlockSpec((1, LANES), lambda i: (0, 0))]
        args += list(router)
        out_specs = [pl.BlockSpec((tr, d), lambda i: (i, 0)), pl.BlockSpec((tr, d // 2), lambda i: (i, 0)),
                     pl.BlockSpec((tr, LANES), lambda i: (i, 0))]
        out_shape = [jax.ShapeDtypeStruct((m, d), F32), jax.ShapeDtypeStruct((m, d // 2), jnp.uint32),
                     jax.ShapeDtypeStruct((m, LANES), F32)]
    return pl.pallas_call(
        functools.partial(_ln_kernel, with_router=router is not None),
        grid=(m // tr,), in_specs=in_specs, out_specs=out_specs, out_shape=out_shape,
        name="layer_norm_router" if router is not None else "layer_norm",
        compiler_params=pltpu.CompilerParams(
            dimension_semantics=("arbitrary",), vmem_limit_bytes=_vmem_limit(tr * d * 36)),
    )(*args)


def _route_kernel(lg_ref, idx_ref, w_ref, rank_ref, cnt_ref, *, n_experts):
    per = n_experts // N_GROUPS
    logit = [lg_ref[e] for e in range(n_experts)]
    mx = functools.reduce(jnp.maximum, logit)
    ex = [jnp.exp(l - mx) for l in logit]
    tot = functools.reduce(lambda a, c: a + c, ex)
    p = [e / tot for e in ex]

    def top2_sum(vals):
        a, bb, c, d = vals
        hi1, lo1 = jnp.maximum(a, bb), jnp.minimum(a, bb)
        hi2, lo2 = jnp.maximum(c, d), jnp.minimum(c, d)
        first = jnp.maximum(hi1, hi2)
        second = jnp.maximum(jnp.minimum(hi1, hi2), jnp.maximum(lo1, lo2))
        return first + second

    assert per == 4
    gs = [top2_sum(p[g * per:(g + 1) * per]) for g in range(N_GROUPS)]
    best = jnp.zeros_like(gs[0], dtype=jnp.int32)
    best_v = gs[0]
    for g in range(1, N_GROUPS):
        upd = gs[g] > best_v
        best = jnp.where(upd, g, best)
        best_v = jnp.where(upd, gs[g], best_v)
    vals = []
    for k in range(per):
        v = p[k]
        for g in range(1, N_GROUPS):
            v = jnp.where(best == g, p[g * per + k], v)
        vals.append(v)
    i1 = jnp.zeros_like(best)
    v1 = vals[0]
    for k in range(1, per):
        upd = vals[k] > v1
        i1 = jnp.where(upd, k, i1)
        v1 = jnp.where(upd, vals[k], v1)
    i2 = jnp.full_like(best, -1)
    v2 = jnp.full_like(v1, -1.0)
    for k in range(per):
        upd = (i1 != k) & (vals[k] > v2)
        i2 = jnp.where(upd, k, i2)
        v2 = jnp.where(upd, vals[k], v2)
    den = v1 + v2
    picks = (best * per + i1, best * per + i2)
    idx_ref[0] = picks[0]
    idx_ref[1] = picks[1]
    w_ref[0] = v1 / den
    w_ref[1] = v2 / den

    rows, lanes = best.shape
    li = lax.broadcasted_iota(jnp.int32, (lanes, lanes), 0)
    lj = lax.broadcasted_iota(jnp.int32, (lanes, lanes), 1)
    upper = jnp.where(li <= lj, 1.0, 0.0).astype(BF16)
    ones = jnp.ones((lanes, lanes), BF16)
    ri = lax.broadcasted_iota(jnp.int32, (rows, rows), 0)
    rj = lax.broadcasted_iota(jnp.int32, (rows, rows), 1)
    before = jnp.where(rj < ri, 1.0, 0.0).astype(BF16)
    ranks = [jnp.zeros((rows, lanes), F32), jnp.zeros((rows, lanes), F32)]
    for e in range(n_experts):
        base = jnp.zeros((1, lanes), F32)
        for s in range(2):
            hit = picks[s] == e
            ind = jnp.where(hit, 1.0, 0.0).astype(BF16)
            in_row = _dot(ind, upper)
            row_tot = _dot(ind, ones)
            row_off = _dot(before, row_tot.astype(BF16))
            ranks[s] = jnp.where(hit, in_row + row_off + (base - 1.0), ranks[s])
            base = base + (row_off + row_tot)[rows - 1:rows, :]
        cnt_ref[pl.ds(e, 1), :] = base.astype(jnp.int32)
    rank_ref[0] = ranks[0].astype(jnp.int32)
    rank_ref[1] = ranks[1].astype(jnp.int32)


def _route(logits_t):
    n_experts, rows, lanes = logits_t.shape
    return pl.pallas_call(
        functools.partial(_route_kernel, n_experts=n_experts),
        out_shape=[jax.ShapeDtypeStruct((2, rows, lanes), jnp.int32),
                   jax.ShapeDtypeStruct((2, rows, lanes), F32),
                   jax.ShapeDtypeStruct((2, rows, lanes), jnp.int32),
                   jax.ShapeDtypeStruct((n_experts, lanes), jnp.int32)],
        name="route",
    )(logits_t)


def _dispatch_kernel(pos_ref, x_ref, init_hbm, o_hbm, sem, *, m):
    del init_hbm
    i = pl.program_id(0)
    chunk = x_ref.shape[0]

    def row_copy(r, dst):
        return pltpu.make_async_copy(x_ref.at[pl.ds(r, 1), :], o_hbm.at[pl.ds(dst, 1), :], sem)

    def issue(r, carry):
        row_copy(r, pos_ref[i * chunk + r]).start()
        row_copy(r, pos_ref[m + i * chunk + r]).start()
        return carry
    lax.fori_loop(0, chunk, issue, 0, unroll=8)

    def drain(r, carry):
        row_copy(r, 0).wait()
        row_copy(r, 0).wait()
        return carry
    lax.fori_loop(0, chunk, drain, 0, unroll=8)


def _dispatch_rows(xp, pos, npad):
    m, n = xp.shape
    chunk = _pick(m, 512, 8)
    return pl.pallas_call(
        functools.partial(_dispatch_kernel, m=m),
        grid_spec=pltpu.PrefetchScalarGridSpec(
            num_scalar_prefetch=1, grid=(m // chunk,),
            in_specs=[pl.BlockSpec((chunk, n), lambda i, pos: (i, 0)), pl.BlockSpec(memory_space=pl.ANY)],
            out_specs=pl.BlockSpec(memory_space=pl.ANY),
            scratch_shapes=[pltpu.SemaphoreType.DMA(())]),
        out_shape=jax.ShapeDtypeStruct((npad, n), xp.dtype),
        input_output_aliases={2: 0},
        name="expert_dispatch",
        compiler_params=pltpu.CompilerParams(dimension_semantics=("arbitrary",)),
    )(pos, xp, jnp.zeros((npad, n), xp.dtype))


def _stream_expert_weights(te_ref, nx_ref, live, w_hbm, stage, w_bf16, sem, layer):
    j, t = pl.program_id(0), pl.program_id(1)
    n_col_tiles = pl.num_programs(0)

    def fetch(k, e, jj):
        tw = stage[k].shape[1]
        src = w_hbm[k].at[layer, e, :, pl.ds(pl.multiple_of(jj * tw, tw), tw)]
        return pltpu.make_async_copy(src, stage[k], sem.at[k])

    @pl.when((j == 0) & (t == 0))
    def _():
        for k in range(len(stage)):
            fetch(k, te_ref[0], 0).start()

    @pl.when(live & ((t == 0) | (te_ref[t] != te_ref[jnp.maximum(t - 1, 0)])))
    def _():
        for k in range(len(stage)):
            fetch(k, te_ref[t], j).wait()
        for k in range(len(stage)):
            w_bf16[k][...] = stage[k][...].astype(BF16)
        nxt = nx_ref[t]

        @pl.when(nxt >= 0)
        def _():
            for k in range(len(stage)):
                fetch(k, nxt, j).start()

        @pl.when((nxt < 0) & (j + 1 < n_col_tiles))
        def _():
            for k in range(len(stage)):
                fetch(k, te_ref[0], j + 1).start()


def _expert_up_kernel(te_ref, tr_ref, nt_ref, nx_ref, x_ref, wg_hbm, wu_hbm, o_ref,
                      wg_st, wu_st, wg_s, wu_s, sem, *, layer):
    t = pl.program_id(1)
    live = t < nt_ref[0]
    n_valid = tr_ref[t]
    tm, half_d = x_ref.shape
    half_m = tm // 2
    _stream_expert_weights(te_ref, nx_ref, live, (wg_hbm, wu_hbm), (wg_st, wu_st), (wg_s, wu_s), sem, layer)

    def compute(nr):
        x_hi, x_lo = _unpack_halves(x_ref[pl.ds(0, nr), :])
        x_hi, x_lo = x_hi.astype(BF16), x_lo.astype(BF16)
        top, bot = pl.ds(0, half_d), pl.ds(half_d, half_d)
        g = _dot(x_hi, wg_s[top, :]) + _dot(x_lo, wg_s[bot, :])
        u = _dot(x_hi, wu_s[top, :]) + _dot(x_lo, wu_s[bot, :])
        o_ref[pl.ds(0, nr), :] = (g * jax.nn.sigmoid(g) * u).astype(o_ref.dtype)

    @pl.when(live & (n_valid > half_m))
    def _():
        compute(tm)

    @pl.when(live & (n_valid <= half_m))
    def _():
        compute(half_m)
        o_ref[pl.ds(half_m, half_m), :] = jnp.zeros((half_m, o_ref.shape[1]), o_ref.dtype)

    @pl.when(jnp.logical_not(live))
    def _():
        o_ref[...] = jnp.zeros_like(o_ref)


def _live_tile(j, t, te, tr, nt, nx):
    return (jnp.minimum(t, nt[0] - 1), 0)


def _expert_up(xg, w_gate, w_up, layer, sched, *, tm):
    npad, half_d = xg.shape
    d = 2 * half_d
    f = w_gate.shape[-1]
    tf = _pick(f, 512)
    return pl.pallas_call(
        functools.partial(_expert_up_kernel, layer=layer),
        grid_spec=pltpu.PrefetchScalarGridSpec(
            num_scalar_prefetch=4, grid=(f // tf, npad // tm),
            in_specs=[pl.BlockSpec((tm, half_d), _live_tile),
                      pl.BlockSpec(memory_space=pl.ANY), pl.BlockSpec(memory_space=pl.ANY)],
            out_specs=pl.BlockSpec((tm, tf), lambda j, t, te, tr, nt, nx: (t, j)),
            scratch_shapes=[pltpu.VMEM((d, tf), F32), pltpu.VMEM((d, tf), F32),
                            pltpu.VMEM((d, tf), BF16), pltpu.VMEM((d, tf), BF16),
                            pltpu.SemaphoreType.DMA((2,))]),
        out_shape=jax.ShapeDtypeStruct((npad, f), BF16),
        name="expert_up",
        compiler_params=pltpu.CompilerParams(
            dimension_semantics=("arbitrary", "arbitrary"),
            vmem_limit_bytes=_vmem_limit(2 * d * tf * 4 + 2 * d * tf * 2 + 5 * tm * d * 2 + 8 * tm * tf * 4)),
    )(*sched, xg, w_gate, w_up)


def _expert_down_kernel(te_ref, tr_ref, nt_ref, nx_ref, h_ref, wd_hbm, o_ref, wd_st, wd_s, sem, *, layer):
    t = pl.program_id(1)
    live = t < nt_ref[0]
    n_valid = tr_ref[t]
    tm = h_ref.shape[0]
    half_m = tm // 2
    _stream_expert_weights(te_ref, nx_ref, live, (wd_hbm,), (wd_st,), (wd_s,), sem, layer)

    def compute(nr):
        o_ref[pl.ds(0, nr), :] = _pack_halves(_dot(h_ref[pl.ds(0, nr), :], wd_s[...]))

    @pl.when(live & (n_valid > half_m))
    def _():
        compute(tm)

    @pl.when(live & (n_valid <= half_m))
    def _():
        compute(half_m)
        o_ref[pl.ds(half_m, half_m), :] = jnp.zeros((half_m, o_ref.shape[1]), o_ref.dtype)

    @pl.when(jnp.logical_not(live))
    def _():
        o_ref[...] = jnp.zeros_like(o_ref)


def _expert_down(hg, w_down, layer, sched, *, tm, tn):
    npad, f = hg.shape
    d = w_down.shape[-1]
    return pl.pallas_call(
        functools.partial(_expert_down_kernel, layer=layer),
        grid_spec=pltpu.PrefetchScalarGridSpec(
            num_scalar_prefetch=4, grid=(d // tn, npad // tm),
            in_specs=[pl.BlockSpec((tm, f), _live_tile), pl.BlockSpec(memory_space=pl.ANY)],
            out_specs=pl.BlockSpec((tm, tn // 2), lambda j, t, te, tr, nt, nx: (t, j)),
            scratch_shapes=[pltpu.VMEM((f, tn), F32), pltpu.VMEM((f, tn), BF16),
                            pltpu.SemaphoreType.DMA((1,))]),
        out_shape=jax.ShapeDtypeStruct((npad, d // 2), jnp.uint32),
        name="expert_down",
        compiler_params=pltpu.CompilerParams(
            dimension_semantics=("arbitrary", "arbitrary"),
            vmem_limit_bytes=_vmem_limit(f * tn * 4 + f * tn * 2 + 2 * tm * f * 2 + 6 * tm * tn * 4)),
    )(*sched, hg, w_down)


def _combine_ln_kernel(pos_ref, y_hbm, x_ref, w0_ref, w1_ref, g_ref, b_ref, o_ref, ob_ref, buf, sem,
                       *, alpha, m, tn):
    i = pl.program_id(0)
    tc, d = x_ref.shape

    def row_copy(slot, s, r, p):
        return pltpu.make_async_copy(y_hbm.at[pl.ds(p, 1), :], buf.at[slot, s, pl.ds(r, 1), :], sem.at[slot])

    def gather(step):
        def issue(r, carry):
            row_copy(step % 2, 0, r, pos_ref[step * tc + r]).start()
            row_copy(step % 2, 1, r, pos_ref[m + step * tc + r]).start()
            return carry
        lax.fori_loop(0, tc, issue, 0, unroll=8)

    @pl.when(i == 0)
    def _():
        gather(i)

    @pl.when(i + 1 < pl.num_programs(0))
    def _():
        gather(i + 1)

    def drain(r, carry):
        row_copy(i % 2, 0, r, 0).wait()
        row_copy(i % 2, 1, r, 0).wait()
        return carry
    lax.fori_loop(0, tc, drain, 0, unroll=8)

    w0, w1 = w0_ref[...], w1_ref[...]
    rows = buf.at[i % 2]
    parts = []
    for j in range(d // tn):
        cs = pl.ds(j * (tn // 2), tn // 2)
        hi0, lo0 = _unpack_halves(rows[0, :, cs])
        hi1, lo1 = _unpack_halves(rows[1, :, cs])
        parts += [w0 * hi0 + w1 * hi1, w0 * lo0 + w1 * lo1]
    z = alpha * x_ref[...] + jnp.concatenate(parts, axis=1)
    mu = jnp.mean(z, axis=-1, keepdims=True)
    dz = z - mu
    var = jnp.mean(dz * dz, axis=-1, keepdims=True)
    x = dz * lax.rsqrt(var + LN_EPS) * g_ref[...] + b_ref[...]
    o_ref[...] = x
    ob_ref[...] = x.astype(BF16)


def _combine_ln(yg, pos, ewts, x, ln_g, ln_b, idx, *, alpha, tn):
    m, d = x.shape
    tc = _pick(m, 256, 8)
    row = lambda i, pos: (i, 0)
    par = lambda i, pos: (idx, 0, 0)
    return pl.pallas_call(
        functools.partial(_combine_ln_kernel, alpha=alpha, m=m, tn=tn),
        grid_spec=pltpu.PrefetchScalarGridSpec(
            num_scalar_prefetch=1, grid=(m // tc,),
            in_specs=[pl.BlockSpec(memory_space=pl.ANY),
                      pl.BlockSpec((tc, d), row),
                      pl.BlockSpec((None, tc, 1), lambda i, pos: (0, i, 0)),
                      pl.BlockSpec((None, tc, 1), lambda i, pos: (1, i, 0)),
                      pl.BlockSpec((None, 1, d), par), pl.BlockSpec((None, 1, d), par)],
            out_specs=[pl.BlockSpec((tc, d), row), pl.BlockSpec((tc, d), row)],
            scratch_shapes=[pltpu.VMEM((2, 2, tc, d // 2), jnp.uint32), pltpu.SemaphoreType.DMA((2,))]),
        out_shape=[jax.ShapeDtypeStruct((m, d), F32), jax.ShapeDtypeStruct((m, d), BF16)],
        name="combine_ln",
        compiler_params=pltpu.CompilerParams(
            dimension_semantics=("arbitrary",), vmem_limit_bytes=_vmem_limit(tc * d * 48)),
    )(pos, yg, x, ewts, ewts, ln_g, ln_b)


def _expert_schedule(eidx, rank, counts, tm, n_tiles_max):
    n_experts = counts.shape[0]
    tiles = (counts + tm - 1) // tm
    tile_end = jnp.cumsum(tiles)
    tile_start = tile_end - tiles
    experts = jnp.arange(n_experts, dtype=jnp.int32)
    pick = eidx[:, :, None] == experts
    pos = rank + jnp.sum(jnp.where(pick, tile_start * tm, 0), axis=-1)
    n_tiles = tile_end[-1]
    tile_ids = jnp.arange(n_tiles_max, dtype=jnp.int32)
    tile_e = jnp.sum(tile_end[None, :] <= jnp.minimum(tile_ids, n_tiles - 1)[:, None], axis=-1)
    mine = tile_e[:, None] == experts
    filled = jnp.sum(jnp.where(mine, counts - (tile_ids[:, None] - tile_start) * tm, 0), axis=-1)
    tile_rows = jnp.clip(filled, 0, tm)
    after = jnp.sum(jnp.where(mine, tile_end, 0), axis=-1)
    expert_after = jnp.sum(tile_end[None, :] <= after[:, None], axis=-1)
    next_e = jnp.where(after < n_tiles, expert_after, -1)
    i32 = lambda a: a.astype(jnp.int32)
    return i32(pos).reshape(-1), (i32(tile_e), i32(tile_rows), i32(n_tiles).reshape(1), i32(next_e))


def kernel(x, p, w_in, b_branch_gate, w_gla_gate_up, b_gla_gate, conv_w, conv_b, conv_ln_g, conv_ln_b, w_conv_out, gla_norm_g, w_gla_out, w_out, ln_g, ln_b, w_router, b_router, w_exp_gate, w_exp_up, w_exp_down, w_ple, w_ple_gate):
    bsz, seq, d = x.shape
    assert bsz == 1
    depth = w_in.shape[0]
    m = bsz * seq
    conv_ch = conv_w.shape[-1]
    rank, dk = w_gla_gate_up.shape[1], w_gla_gate_up.shape[2]
    dv = gla_norm_g.shape[-1]
    n_experts = w_router.shape[1]
    off_b = conv_ch
    off_q = 2 * conv_ch
    off_alpha = off_q + 2 * dk + 2 * dv
    off_gate = off_alpha + rank
    alpha = (2.0 * depth) ** 0.25
    tm_e = 512 if m >= 4096 else 64
    n_tiles_max = (2 * m) // tm_e + n_experts
    tn_down = _pick(d, 2048)

    xf = x.reshape(m, d)
    xb = xf.astype(BF16)
    pb = p.reshape(depth, m, p.shape[-1]).astype(BF16)
    w_in_t = jnp.swapaxes(w_in, 1, 2)
    b_gate = b_branch_gate.reshape(depth, 1, 2 * d)
    wup_pad = jnp.zeros((depth, LANES, dk), F32).at[:, :rank, :].set(w_gla_gate_up)
    b_a = b_gla_gate.reshape(depth, 1, dk)
    conv_b3 = conv_b.reshape(depth, 1, conv_ch)
    cln_g = conv_ln_g.reshape(depth, 1, conv_ch)
    cln_b = conv_ln_b.reshape(depth, 1, conv_ch)
    gnorm = gla_norm_g.reshape(depth, 1, dv)
    ln_g3 = ln_g.reshape(depth * 3, 1, d)
    ln_b3 = ln_b.reshape(depth * 3, 1, d)
    wr_pad = jnp.zeros((d, LANES), F32).at[:, :n_experts].set(w_router)
    br_pad = jnp.zeros((1, LANES), F32).at[0, :n_experts].set(b_router)

    tm = _pick(m, 1024, 8)
    for i in range(depth):
        (u,) = _fused_matmul(
            [xb], [(0, w_in_t, i, 0, True), (0, w_in_t, i, off_b, True)], [],
            lambda ds, ex: [ds[0] * jax.nn.sigmoid(ds[1])], [F32],
            n_cols=conv_ch, tm=tm, tn=_pick(conv_ch, 256), name="in_proj_glu")
        (h2,) = _fused_matmul(
            [xb], [(0, w_in_t, i, off_q, True)], [], lambda ds, ex: [ds[0]], [F32],
            n_cols=2 * dk + 2 * dv, tm=tm, tn=_pick(dk, 512), name="in_proj_qkvr")
        (a_low,) = _fused_matmul(
            [xb], [(0, w_in_t, i, off_alpha, True)], [], lambda ds, ex: [ds[0]], [F32],
            n_cols=LANES, tm=tm, tn=LANES, name="in_proj_alpha")
        (gates,) = _fused_matmul(
            [xb], [(0, w_in_t, i, off_gate, True)], [('row', b_gate, i, 0)],
            lambda ds, ex: [jax.nn.sigmoid(ds[0] + ex[0])], [BF16],
            n_cols=2 * d, tm=tm, tn=_pick(d, 512), name="in_proj_gates")

        uc = _conv_ln_silu(u, conv_w, conv_b3, cln_g, cln_b, i)
        og = _gla(h2, a_low, wup_pad, b_a, gnorm, i, dk=dk, dv=dv, rank=rank)

        tn_o = _pick(d, 512)
        (merged,) = _fused_matmul(
            [uc, og], [(0, w_conv_out, i, 0, False), (1, w_gla_out, i, 0, False)],
            [('tile', gates, None, 0), ('tile', gates, None, d)],
            lambda ds, ex: [ex[0].astype(F32) * ds[0] + ex[1].astype(F32) * ds[1]], [BF16],
            n_cols=d, tm=tm, tn=tn_o, name="branch_out_merge")
        (z1,) = _fused_matmul(
            [merged], [(0, w_out, i, 0, False)], [('tile', xf, None, 0)],
            lambda ds, ex: [alpha * ex[0] + ds[0]], [F32],
            n_cols=d, tm=tm, tn=tn_o, name="mixer_out")
        x1, x1p, logits = _layer_norm(z1, ln_g3, ln_b3, 3 * i, router=(wr_pad, br_pad))

        logits_t = logits[:, :n_experts].T.reshape(n_experts, m // LANES, LANES)
        eidx, ewts, pair_rank, counts = _route(logits_t)
        pos, sched = _expert_schedule(
            eidx.reshape(2, m), pair_rank.reshape(2, m), counts[:, 0], tm_e, n_tiles_max)
        xg = _dispatch_rows(x1p, pos, n_tiles_max * tm_e)
        hg = _expert_up(xg, w_exp_gate, w_exp_up, i, sched, tm=tm_e)
        yg = _expert_down(hg, w_exp_down, i, sched, tm=tm_e, tn=tn_down)
        x2, x2b = _combine_ln(yg, pos, ewts.reshape(2, m, 1), x1, ln_g3, ln_b3, 3 * i + 1,
                              alpha=alpha, tn=tn_down)

        (z3,) = _fused_matmul(
            [x2b, pb[i]], [(0, w_ple_gate, i, 0, False), (1, w_ple, i, 0, False)], [('tile', x2, None, 0)],
            lambda ds, ex: [alpha * ex[0] + jax.nn.sigmoid(ds[0]) * ds[1]], [F32],
            n_cols=d, tm=tm, tn=tn_o, name="ple")
        xf, xb = _layer_norm(z3, ln_g3, ln_b3, 3 * i + 2)

    return xf.reshape(bsz, seq, d)
```

```python
import functools

import jax
import jax.numpy as jnp
from jax import lax
from jax.experimental import pallas as pl
from jax.experimental.pallas import tpu as pltpu

F32 = jnp.float32
BF16 = jnp.bfloat16

GLA_HEADS = 4
GLA_GATE_TEMP = 16.0
GLA_CHUNK = 64
N_GROUPS = 4
LN_EPS = 1e-5
LANES = 128
V7X_SCOPED_VMEM_BYTES = 60000 * 1024


def _pick(n, pref, mult=LANES):
    if n <= pref:
        return n
    t = (pref // mult) * mult
    while t >= mult:
        if n % t == 0:
            return t
        t -= mult
    return n


def _vmem_limit(nbytes):
    return int(min(max(nbytes * 5 // 4 + (4 << 20), 16 << 20), V7X_SCOPED_VMEM_BYTES))


def _split3(x):
    hi = x.astype(BF16)
    r1 = x - hi.astype(F32)
    mid = r1.astype(BF16)
    lo = (r1 - mid.astype(F32)).astype(BF16)
    return hi, mid, lo


def _dot(a, b):
    return jnp.dot(a, b, preferred_element_type=F32)


def _dot_f32(a, b):
    ah, am, _ = _split3(a)
    bh, bm, _ = _split3(b)
    return _dot(ah, bh) + (_dot(ah, bm) + _dot(am, bh))


NT_DIMS = (((1,), (1,)), ((), ()))


def _fused_matmul_kernel(*refs, n_acts, dot_act, forms, n_extras, n_outs, epilogue):
    n_w = sum(2 if shift else 1 for _, shift in forms)
    act_refs = refs[:n_acts]
    w_refs = list(refs[n_acts:n_acts + n_w])
    ex_refs = refs[n_acts + n_w:n_acts + n_w + n_extras]
    out_refs = refs[n_acts + n_w + n_extras:n_acts + n_w + n_extras + n_outs]
    w_bf16 = refs[n_acts + n_w + n_extras + n_outs:]

    @pl.when(pl.program_id(1) == 0)
    def _():
        for s, (_, shift) in zip(w_bf16, forms):
            w_ref = w_refs.pop(0)
            if shift:
                tn = s.shape[0]
                s[pl.ds(0, tn - shift), :] = w_ref[pl.ds(shift, tn - shift), :].astype(BF16)
                s[pl.ds(tn - shift, shift), :] = w_refs.pop(0)[...].astype(BF16)
            else:
                s[...] = w_ref[...].astype(BF16)

    dots = []
    for ai, s, (output_major, _) in zip(dot_act, w_bf16, forms):
        a = act_refs[ai][...]
        dots.append(lax.dot_general(a, s[...], NT_DIMS, preferred_element_type=F32) if output_major
                    else _dot(a, s[...]))
    outs = epilogue(dots, [e[...] for e in ex_refs])
    for o_ref, o in zip(out_refs, outs):
        o_ref[...] = o.astype(o_ref.dtype)


def _fused_matmul(acts, dots, extras, epilogue, out_dtypes, *, n_cols, tm, tn, name):
    m = acts[0].shape[0]
    assert m % tm == 0 and n_cols % tn == 0
    grid = (n_cols // tn, m // tm)
    in_specs, args, scratch, forms = [], [], [], []
    nbytes = 0
    for a in acts:
        k = a.shape[1]
        in_specs.append(pl.BlockSpec((tm, k), lambda j, i: (i, 0)))
        args.append(a)
        nbytes += 2 * tm * k * a.dtype.itemsize
    for (_, w, layer, off, output_major) in dots:
        shift = off % tn
        if output_major:
            k = w.shape[2]
            in_specs.append(pl.BlockSpec((None, tn, k), lambda j, i, layer=layer, cb=off // tn: (layer, cb + j, 0)))
            args.append(w)
            if shift:
                assert shift % 16 == 0 and tn % shift == 0 and (off - shift) % shift == 0
                in_specs.append(pl.BlockSpec(
                    (None, shift, k),
                    lambda j, i, layer=layer, cb=off // tn, r=tn // shift: (layer, (cb + j + 1) * r, 0)))
                args.append(w)
                nbytes += 2 * shift * k * 4
            scratch.append(pltpu.VMEM((tn, k), BF16))
        else:
            assert shift == 0
            k = w.shape[1]
            in_specs.append(pl.BlockSpec((None, k, tn), lambda j, i, layer=layer, cb=off // tn: (layer, 0, cb + j)))
            args.append(w)
            scratch.append(pltpu.VMEM((k, tn), BF16))
        forms.append((output_major, shift))
        nbytes += 2 * k * tn * 4 + k * tn * 2
    for (kind, arr, layer, off) in extras:
        assert off % tn == 0
        if kind == 'tile':
            in_specs.append(pl.BlockSpec((tm, tn), lambda j, i, cb=off // tn: (i, cb + j)))
            nbytes += 2 * tm * tn * arr.dtype.itemsize
        else:
            in_specs.append(pl.BlockSpec((None, 1, tn), lambda j, i, layer=layer, cb=off // tn: (layer, 0, cb + j)))
        args.append(arr)
    out_specs = [pl.BlockSpec((tm, tn), lambda j, i: (i, j)) for _ in out_dtypes]
    out_shape = [jax.ShapeDtypeStruct((m, n_cols), dt) for dt in out_dtypes]
    nbytes += sum(2 * tm * tn * jnp.dtype(dt).itemsize for dt in out_dtypes)
    nbytes += (len(dots) + 3) * tm * tn * 4
    kernel = functools.partial(
        _fused_matmul_kernel, n_acts=len(acts), dot_act=tuple(d[0] for d in dots), forms=tuple(forms),
        n_extras=len(extras), n_outs=len(out_dtypes), epilogue=epilogue)
    return pl.pallas_call(
        kernel, grid=grid, in_specs=in_specs, out_specs=out_specs, out_shape=out_shape,
        scratch_shapes=scratch, name=name,
        compiler_params=pltpu.CompilerParams(
            dimension_semantics=("arbitrary", "arbitrary"), vmem_limit_bytes=_vmem_limit(nbytes)),
    )(*args)


def _conv_kernel(u_ref, prev_ref, w_ref, b_ref, g_ref, beta_ref, o_ref, win, shifted, ybuf, *, kw, halo, rows):
    i = pl.program_id(0)
    j = pl.program_id(1)
    tt, cc = u_ref.shape
    win[pl.ds(0, halo), :] = jnp.where(i > 0, prev_ref[...], 0.0)
    win[pl.ds(halo, tt), :] = u_ref[...]
    span = shifted.shape[1]
    for s in range(1, 8):
        shifted[s - 1] = win[pl.ds(s, span), :]
    lead = halo - (kw - 1)
    for r0 in range(0, tt, rows):
        acc = jnp.broadcast_to(b_ref[...], (rows, cc))
        for t in range(kw):
            s = (lead + t) % 8
            start = r0 + lead + t - s
            src = win[pl.ds(start, rows), :] if s == 0 else shifted[s - 1, pl.ds(start, rows), :]
            acc = acc + src * w_ref[pl.ds(t, 1), :]
        ybuf[j, pl.ds(r0, rows), :] = acc

    @pl.when(j == pl.num_programs(1) - 1)
    def _():
        nc = ybuf.shape[0]
        c = nc * cc
        s = jnp.zeros((tt, 1), F32)
        for k in range(nc):
            s = s + jnp.sum(ybuf[k], axis=-1, keepdims=True)
        mu = s * (1.0 / c)
        v = jnp.zeros((tt, 1), F32)
        for k in range(nc):
            d = ybuf[k] - mu
            v = v + jnp.sum(d * d, axis=-1, keepdims=True)
        rstd = lax.rsqrt(v * (1.0 / c) + LN_EPS)
        for k in range(nc):
            cs = slice(k * cc, (k + 1) * cc)
            y = (ybuf[k] - mu) * rstd * g_ref[:, cs] + beta_ref[:, cs]
            o_ref[:, cs] = (y * jax.nn.sigmoid(y)).astype(o_ref.dtype)


def _conv_ln_silu(u, conv_w, conv_b, ln_g, ln_b, layer):
    m, c = u.shape
    kw = conv_w.shape[1]
    halo = -(-(kw - 1) // 8) * 8
    tt = _pick(m, 512, halo)
    cc = _pick(c, 256)
    rows = 64 if tt % 64 == 0 else tt
    kernel = functools.partial(_conv_kernel, kw=kw, halo=halo, rows=rows)
    return pl.pallas_call(
        kernel, grid=(m // tt, c // cc),
        in_specs=[
            pl.BlockSpec((tt, cc), lambda i, j: (i, j)),
            pl.BlockSpec((halo, cc), lambda i, j: (jnp.maximum(i * (tt // halo) - 1, 0), j)),
            pl.BlockSpec((None, kw, cc), lambda i, j: (layer, 0, j)),
            pl.BlockSpec((None, 1, cc), lambda i, j: (layer, 0, j)),
            pl.BlockSpec((None, 1, c), lambda i, j: (layer, 0, 0)),
            pl.BlockSpec((None, 1, c), lambda i, j: (layer, 0, 0)),
        ],
        out_specs=pl.BlockSpec((tt, c), lambda i, j: (i, 0)),
        out_shape=jax.ShapeDtypeStruct((m, c), BF16),
        scratch_shapes=[pltpu.VMEM((tt + halo, cc), F32), pltpu.VMEM((7, tt + halo - 8, cc), F32),
                        pltpu.VMEM((c // cc, tt, cc), F32)],
        name="conv_ln_silu",
        compiler_params=pltpu.CompilerParams(
            dimension_semantics=("arbitrary", "arbitrary"),
            vmem_limit_bytes=_vmem_limit(3 * tt * c * 4 + 14 * (tt + halo) * cc * 4)),
    )(u, u, conv_w, conv_b, ln_g, ln_b)


def _gla_kernel(q_ref, k_ref, v_ref, r_ref, a_ref, wup_ref, ba_ref, g_ref, o_ref,
                state, qe_s, qi_s, ki_s, kd_s, dec_s, obuf, *, rank, chunk):
    tb, dk = q_ref.shape
    dv = v_ref.shape[1]
    hk, hv = dk // GLA_HEADS, dv // GLA_HEADS
    nchunks = tb // chunk

    @pl.when(pl.program_id(0) == 0)
    def _():
        state[...] = jnp.zeros_like(state)

    lane = lax.broadcasted_iota(jnp.int32, a_ref.shape, 1)
    a_low = jnp.where(lane < rank, a_ref[...], 0.0)
    z = _dot_f32(a_low, wup_ref[...]) + ba_ref[...]
    la = (jnp.minimum(z, 0.0) - jnp.log(1.0 + jnp.exp(-jnp.abs(z)))) * (1.0 / GLA_GATE_TEMP)
    la3 = _split3(la)

    rr = lax.broadcasted_iota(jnp.int32, (tb, tb), 0)
    cc = lax.broadcasted_iota(jnp.int32, (tb, tb), 1)
    sh = chunk.bit_length() - 1
    assert chunk == 1 << sh
    same = (rr >> sh) == (cc >> sh)
    tril = jnp.where(same & (cc <= rr), 1.0, 0.0).astype(BF16)
    ones_blk = jnp.where(same, 1.0, 0.0).astype(BF16)
    b = _dot(tril, la3[0]) + (_dot(tril, la3[1]) + _dot(tril, la3[2]))
    bl = _dot(ones_blk, la3[0]) + (_dot(ones_blk, la3[1]) + _dot(ones_blk, la3[2]))

    sr = lax.broadcasted_iota(jnp.int32, (tb, nchunks * LANES), 0)
    sc = lax.broadcasted_iota(jnp.int32, (tb, nchunks * LANES), 1)
    sel = jnp.where((sr >> sh) == (sc >> (LANES.bit_length() - 1)), 1.0, 0.0).astype(BF16)
    tn_dims = (((0,), (0,)), ((), ()))
    tot_t = (lax.dot_general(la3[0], sel, tn_dims, preferred_element_type=F32)
             + (lax.dot_general(la3[1], sel, tn_dims, preferred_element_type=F32)
                + lax.dot_general(la3[2], sel, tn_dims, preferred_element_type=F32)))
    dec_s[...] = jnp.exp(tot_t)

    q = q_ref[...].astype(F32) * (float(hk) ** -0.5)
    k = k_ref[...].astype(F32)
    half = 0.5 * bl
    qe_s[...] = (q * jnp.exp(b)).astype(BF16)
    qi_s[...] = (q * jnp.exp(b - half)).astype(BF16)
    ki_s[...] = (k * jnp.exp(half - b)).astype(BF16)
    kd_s[...] = (k * jnp.exp(bl - b)).astype(BF16)

    ri = lax.broadcasted_iota(jnp.int32, (chunk, chunk), 0)
    ci = lax.broadcasted_iota(jnp.int32, (chunk, chunk), 1)
    causal = ci <= ri
    nt_dims = (((1,), (1,)), ((), ()))
    for c in range(nchunks):
        rows = pl.ds(c * chunk, chunk)
        for h in range(GLA_HEADS):
            ks = pl.ds(h * hk, hk)
            vs = pl.ds(h * hv, hv)
            vch = v_ref[rows, vs].astype(BF16)
            scores = lax.dot_general(qi_s[rows, ks], ki_s[rows, ks], nt_dims, preferred_element_type=F32)
            scores = jnp.where(causal, scores, 0.0).astype(BF16)
            s_old = state[h]
            o = _dot(qe_s[rows, ks], s_old.astype(BF16)) + _dot(scores, vch)
            obuf[rows, vs] = o
            dcol = dec_s[ks, pl.ds(c * LANES, LANES)]
            decay = jnp.concatenate([dcol] * (hv // LANES), axis=1) if hv >= LANES else dcol[:, :hv]
            state[h] = s_old * decay + lax.dot_general(kd_s[rows, ks], vch, tn_dims, preferred_element_type=F32)

    for h in range(GLA_HEADS):
        vs = pl.ds(h * hv, hv)
        o = obuf[:, vs]
        ms = jnp.mean(o * o, axis=-1, keepdims=True)
        on = o * lax.rsqrt(ms + LN_EPS) * g_ref[:, vs]
        r = r_ref[:, vs].astype(F32)
        o_ref[:, vs] = (on * (r * jax.nn.sigmoid(r))).astype(o_ref.dtype)


def _gla(h2, a_low, wup_pad, b_a, norm_g, layer, *, dk, dv, rank):
    m = h2.shape[0]
    tb = _pick(m, 256, GLA_CHUNK)
    assert tb % GLA_CHUNK == 0 and dv == 2 * dk
    nchunks = tb // GLA_CHUNK
    hk, hv = dk // GLA_HEADS, dv // GLA_HEADS
    kernel = functools.partial(_gla_kernel, rank=rank, chunk=GLA_CHUNK)
    return pl.pallas_call(
        kernel, grid=(m // tb,),
        in_specs=[
            pl.BlockSpec((tb, dk), lambda i: (i, 0)),
            pl.BlockSpec((tb, dk), lambda i: (i, 1)),
            pl.BlockSpec((tb, dv), lambda i: (i, 1)),
            pl.BlockSpec((tb, dv), lambda i: (i, 2)),
            pl.BlockSpec((tb, LANES), lambda i: (i, 0)),
            pl.BlockSpec((None, LANES, dk), lambda i: (layer, 0, 0)),
            pl.BlockSpec((None, 1, dk), lambda i: (layer, 0, 0)),
            pl.BlockSpec((None, 1, dv), lambda i: (layer, 0, 0)),
        ],
        out_specs=pl.BlockSpec((tb, dv), lambda i: (i, 0)),
        out_shape=jax.ShapeDtypeStruct((m, dv), BF16),
        scratch_shapes=[
            pltpu.VMEM((GLA_HEADS, hk, hv), F32),
            pltpu.VMEM((tb, dk), BF16), pltpu.VMEM((tb, dk), BF16),
            pltpu.VMEM((tb, dk), BF16), pltpu.VMEM((tb, dk), BF16),
            pltpu.VMEM((dk, nchunks * LANES), F32),
            pltpu.VMEM((tb, dv), F32),
        ],
        name="gla",
        compiler_params=pltpu.CompilerParams(
            dimension_semantics=("arbitrary",), vmem_limit_bytes=_vmem_limit(48 << 20)),
    )(h2, h2, h2, h2, a_low, wup_pad, b_a, norm_g)


def _pack_halves(x):
    n = x.shape[1] // 2
    hi = lax.bitcast_convert_type(x[:, :n].astype(BF16).astype(F32), jnp.uint32)
    lo = lax.bitcast_convert_type(x[:, n:].astype(BF16).astype(F32), jnp.uint32)
    return hi | (lo >> 16)


def _unpack_halves(w):
    hi = lax.bitcast_convert_type(w & jnp.uint32(0xFFFF0000), F32)
    lo = lax.bitcast_convert_type(w << 16, F32)
    return hi, lo


def _ln_kernel(z_ref, g_ref, b_ref, *rest, with_router):
    if with_router:
        wr_ref, br_ref, x_ref, xp_ref, lg_ref = rest
    else:
        x_ref, xb_ref = rest
    z = z_ref[...]
    mu = jnp.mean(z, axis=-1, keepdims=True)
    d = z - mu
    var = jnp.mean(d * d, axis=-1, keepdims=True)
    x = d * lax.rsqrt(var + LN_EPS) * g_ref[...] + b_ref[...]
    x_ref[...] = x
    if with_router:
        xp_ref[...] = _pack_halves(x)
        lg_ref[...] = _dot_f32(x, wr_ref[...]) + br_ref[...]
    else:
        xb_ref[...] = x.astype(BF16)


def _layer_norm(z, ln_g, ln_b, idx, router=None):
    m, d = z.shape
    tr = _pick(m, 512, 8)
    in_specs = [
        pl.BlockSpec((tr, d), lambda i: (i, 0)),
        pl.BlockSpec((None, 1, d), lambda i: (idx, 0, 0)),
        pl.BlockSpec((None, 1, d), lambda i: (idx, 0, 0)),
    ]
    args = [z, ln_g, ln_b]
    if router is None:
        out_specs = [pl.BlockSpec((tr, d), lambda i: (i, 0)), pl.BlockSpec((tr, d), lambda i: (i, 0))]
        out_shape = [jax.ShapeDtypeStruct((m, d), F32), jax.ShapeDtypeStruct((m, d), BF16)]
    else:
        in_specs += [pl.BlockSpec((d, LANES), lambda i: (0, 0)), pl.BlockSpec((1, LANES), lambda i: (0, 0))]
        args += list(router)
        out_specs = [pl.BlockSpec((tr, d), lambda i: (i, 0)), pl.BlockSpec((tr, d // 2), lambda i: (i, 0)),
                     pl.BlockSpec((tr, LANES), lambda i: (i, 0))]
        out_shape = [jax.ShapeDtypeStruct((m, d), F32), jax.ShapeDtypeStruct((m, d // 2), jnp.uint32),
                     jax.ShapeDtypeStruct((m, LANES), F32)]
    return pl.pallas_call(
        functools.partial(_ln_kernel, with_router=router is not None),
        grid=(m // tr,), in_specs=in_specs, out_specs=out_specs, out_shape=out_shape,
        name="layer_norm_router" if router is not None else "layer_norm",
        compiler_params=pltpu.CompilerParams(
            dimension_semantics=("arbitrary",), vmem_limit_bytes=_vmem_limit(tr * d * 36)),
    )(*args)


def _route_kernel(lg_ref, idx_ref, w_ref, rank_ref, cnt_ref, *, n_experts):
    per = n_experts // N_GROUPS
    logit = [lg_ref[e] for e in range(n_experts)]
    mx = functools.reduce(jnp.maximum, logit)
    ex = [jnp.exp(l - mx) for l in logit]
    tot = functools.reduce(lambda a, c: a + c, ex)
    p = [e / tot for e in ex]

    def top2_sum(vals):
        a, bb, c, d = vals
        hi1, lo1 = jnp.maximum(a, bb), jnp.minimum(a, bb)
        hi2, lo2 = jnp.maximum(c, d), jnp.minimum(c, d)
        first = jnp.maximum(hi1, hi2)
        second = jnp.maximum(jnp.minimum(hi1, hi2), jnp.maximum(lo1, lo2))
        return first + second

    assert per == 4
    gs = [top2_sum(p[g * per:(g + 1) * per]) for g in range(N_GROUPS)]
    best = jnp.zeros_like(gs[0], dtype=jnp.int32)
    best_v = gs[0]
    for g in range(1, N_GROUPS):
        upd = gs[g] > best_v
        best = jnp.where(upd, g, best)
        best_v = jnp.where(upd, gs[g], best_v)
    vals = []
    for k in range(per):
        v = p[k]
        for g in range(1, N_GROUPS):
            v = jnp.where(best == g, p[g * per + k], v)
        vals.append(v)
    i1 = jnp.zeros_like(best)
    v1 = vals[0]
    for k in range(1, per):
        upd = vals[k] > v1
        i1 = jnp.where(upd, k, i1)
        v1 = jnp.where(upd, vals[k], v1)
    i2 = jnp.full_like(best, -1)
    v2 = jnp.full_like(v1, -1.0)
    for k in range(per):
        upd = (i1 != k) & (vals[k] > v2)
        i2 = jnp.where(upd, k, i2)
        v2 = jnp.where(upd, vals[k], v2)
    den = v1 + v2
    picks = (best * per + i1, best * per + i2)
    idx_ref[0] = picks[0]
    idx_ref[1] = picks[1]
    w_ref[0] = v1 / den
    w_ref[1] = v2 / den

    rows, lanes = best.shape
    li = lax.broadcasted_iota(jnp.int32, (lanes, lanes), 0)
    lj = lax.broadcasted_iota(jnp.int32, (lanes, lanes), 1)
    upper = jnp.where(li <= lj, 1.0, 0.0).astype(BF16)
    ones = jnp.ones((lanes, lanes), BF16)
    ri = lax.broadcasted_iota(jnp.int32, (rows, rows), 0)
    rj = lax.broadcasted_iota(jnp.int32, (rows, rows), 1)
    before = jnp.where(rj < ri, 1.0, 0.0).astype(BF16)
    ranks = [jnp.zeros((rows, lanes), F32), jnp.zeros((rows, lanes), F32)]
    for e in range(n_experts):
        base = jnp.zeros((1, lanes), F32)
        for s in range(2):
            hit = picks[s] == e
            ind = jnp.where(hit, 1.0, 0.0).astype(BF16)
            in_row = _dot(ind, upper)
            row_tot = _dot(ind, ones)
            row_off = _dot(before, row_tot.astype(BF16))
            ranks[s] = jnp.where(hit, in_row + row_off + (base - 1.0), ranks[s])
            base = base + (row_off + row_tot)[rows - 1:rows, :]
        cnt_ref[pl.ds(e, 1), :] = base.astype(jnp.int32)
    rank_ref[0] = ranks[0].astype(jnp.int32)
    rank_ref[1] = ranks[1].astype(jnp.int32)


def _route(logits_t):
    n_experts, rows, lanes = logits_t.shape
    return pl.pallas_call(
        functools.partial(_route_kernel, n_experts=n_experts),
        out_shape=[jax.ShapeDtypeStruct((2, rows, lanes), jnp.int32),
                   jax.ShapeDtypeStruct((2, rows, lanes), F32),
                   jax.ShapeDtypeStruct((2, rows, lanes), jnp.int32),
                   jax.ShapeDtypeStruct((n_experts, lanes), jnp.int32)],
        name="route",
    )(logits_t)


def _dispatch_kernel(pos_ref, x_ref, init_hbm, o_hbm, sem, *, m):
    del init_hbm
    i = pl.program_id(0)
    chunk = x_ref.shape[0]

    def row_copy(r, dst):
        return pltpu.make_async_copy(x_ref.at[pl.ds(r, 1), :], o_hbm.at[pl.ds(dst, 1), :], sem)

    def issue(r, carry):
        row_copy(r, pos_ref[i * chunk + r]).start()
        row_copy(r, pos_ref[m + i * chunk + r]).start()
        return carry
    lax.fori_loop(0, chunk, issue, 0, unroll=8)

    def drain(r, carry):
        row_copy(r, 0).wait()
        row_copy(r, 0).wait()
        return carry
    lax.fori_loop(0, chunk, drain, 0, unroll=8)


def _dispatch_rows(xp, pos, npad):
    m, n = xp.shape
    chunk = _pick(m, 512, 8)
    return pl.pallas_call(
        functools.partial(_dispatch_kernel, m=m),
        grid_spec=pltpu.PrefetchScalarGridSpec(
            num_scalar_prefetch=1, grid=(m // chunk,),
            in_specs=[pl.BlockSpec((chunk, n), lambda i, pos: (i, 0)), pl.BlockSpec(memory_space=pl.ANY)],
            out_specs=pl.BlockSpec(memory_space=pl.ANY),
            scratch_shapes=[pltpu.SemaphoreType.DMA(())]),
        out_shape=jax.ShapeDtypeStruct((npad, n), xp.dtype),
        input_output_aliases={2: 0},
        name="expert_dispatch",
        compiler_params=pltpu.CompilerParams(dimension_semantics=("arbitrary",)),
    )(pos, xp, jnp.zeros((npad, n), xp.dtype))


def _stream_expert_weights(te_ref, nx_ref, live, w_hbm, stage, w_bf16, sem, layer):
    j, t = pl.program_id(0), pl.program_id(1)
    n_col_tiles = pl.num_programs(0)

    def fetch(k, e, jj):
        tw = stage[k].shape[1]
        src = w_hbm[k].at[layer, e, :, pl.ds(pl.multiple_of(jj * tw, tw), tw)]
        return pltpu.make_async_copy(src, stage[k], sem.at[k])

    @pl.when((j == 0) & (t == 0))
    def _():
        for k in range(len(stage)):
            fetch(k, te_ref[0], 0).start()

    @pl.when(live & ((t == 0) | (te_ref[t] != te_ref[jnp.maximum(t - 1, 0)])))
    def _():
        for k in range(len(stage)):
            fetch(k, te_ref[t], j).wait()
        rows = stage[0].shape[0]
        step = _pick(rows, 512, 16)

        def convert(c, carry):
            rs = pl.ds(pl.multiple_of(c * step, step), step)
            for k in range(len(stage)):
                w_bf16[k][rs, :] = stage[k][rs, :].astype(BF16)
            return carry
        lax.fori_loop(0, rows // step, convert, 0)
        nxt = nx_ref[t]

        @pl.when(nxt >= 0)
        def _():
            for k in range(len(stage)):
                fetch(k, nxt, j).start()

        @pl.when((nxt < 0) & (j + 1 < n_col_tiles))
        def _():
            for k in range(len(stage)):
                fetch(k, te_ref[0], j + 1).start()


ROW_PARTS = 4


def _compute_valid_rows(live, n_valid, compute, o_ref):
    tm = o_ref.shape[0]
    part = tm // ROW_PARTS
    for nr in range(part, tm + 1, part):
        @pl.when(live & (n_valid > nr - part) & (n_valid <= nr))
        def _(nr=nr):
            compute(nr)
            if nr < tm:
                o_ref[pl.ds(nr, tm - nr), :] = jnp.zeros((tm - nr, o_ref.shape[1]), o_ref.dtype)

    @pl.when(jnp.logical_not(live))
    def _():
        o_ref[...] = jnp.zeros_like(o_ref)


def _expert_up_kernel(te_ref, tr_ref, nt_ref, nx_ref, x_ref, wg_hbm, wu_hbm, o_ref,
                      wg_st, wu_st, wg_s, wu_s, sem, *, layer):
    t = pl.program_id(1)
    live = t < nt_ref[0]
    n_valid = tr_ref[t]
    half_d = x_ref.shape[1]
    _stream_expert_weights(te_ref, nx_ref, live, (wg_hbm, wu_hbm), (wg_st, wu_st), (wg_s, wu_s), sem, layer)

    def compute(nr):
        x_hi, x_lo = _unpack_halves(x_ref[pl.ds(0, nr), :])
        x_hi, x_lo = x_hi.astype(BF16), x_lo.astype(BF16)
        top, bot = pl.ds(0, half_d), pl.ds(half_d, half_d)
        g = _dot(x_hi, wg_s[top, :]) + _dot(x_lo, wg_s[bot, :])
        u = _dot(x_hi, wu_s[top, :]) + _dot(x_lo, wu_s[bot, :])
        o_ref[pl.ds(0, nr), :] = (g * jax.nn.sigmoid(g) * u).astype(o_ref.dtype)

    _compute_valid_rows(live, n_valid, compute, o_ref)


def _live_tile(j, t, te, tr, nt, nx):
    return (jnp.minimum(t, nt[0] - 1), 0)


def _expert_up(xg, w_gate, w_up, layer, sched, *, tm):
    npad, half_d = xg.shape
    d = 2 * half_d
    f = w_gate.shape[-1]
    tf = _pick(f, 512)
    return pl.pallas_call(
        functools.partial(_expert_up_kernel, layer=layer),
        grid_spec=pltpu.PrefetchScalarGridSpec(
            num_scalar_prefetch=4, grid=(f // tf, npad // tm),
            in_specs=[pl.BlockSpec((tm, half_d), _live_tile),
                      pl.BlockSpec(memory_space=pl.ANY), pl.BlockSpec(memory_space=pl.ANY)],
            out_specs=pl.BlockSpec((tm, tf), lambda j, t, te, tr, nt, nx: (t, j)),
            scratch_shapes=[pltpu.VMEM((d, tf), F32), pltpu.VMEM((d, tf), F32),
                            pltpu.VMEM((d, tf), BF16), pltpu.VMEM((d, tf), BF16),
                            pltpu.SemaphoreType.DMA((2,))]),
        out_shape=jax.ShapeDtypeStruct((npad, f), BF16),
        name="expert_up",
        compiler_params=pltpu.CompilerParams(
            dimension_semantics=("arbitrary", "arbitrary"),
            vmem_limit_bytes=_vmem_limit(2 * d * tf * 4 + 2 * d * tf * 2 + 5 * tm * d * 2 + 8 * tm * tf * 4)),
    )(*sched, xg, w_gate, w_up)


def _expert_down_kernel(te_ref, tr_ref, nt_ref, nx_ref, h_ref, wd_hbm, o_ref, wd_st, wd_s, sem, *, layer):
    t = pl.program_id(1)
    live = t < nt_ref[0]
    n_valid = tr_ref[t]
    _stream_expert_weights(te_ref, nx_ref, live, (wd_hbm,), (wd_st,), (wd_s,), sem, layer)

    def compute(nr):
        o_ref[pl.ds(0, nr), :] = _pack_halves(_dot(h_ref[pl.ds(0, nr), :], wd_s[...]))

    _compute_valid_rows(live, n_valid, compute, o_ref)


def _expert_down(hg, w_down, layer, sched, *, tm, tn):
    npad, f = hg.shape
    d = w_down.shape[-1]
    return pl.pallas_call(
        functools.partial(_expert_down_kernel, layer=layer),
        grid_spec=pltpu.PrefetchScalarGridSpec(
            num_scalar_prefetch=4, grid=(d // tn, npad // tm),
            in_specs=[pl.BlockSpec((tm, f), _live_tile), pl.BlockSpec(memory_space=pl.ANY)],
            out_specs=pl.BlockSpec((tm, tn // 2), lambda j, t, te, tr, nt, nx: (t, j)),
            scratch_shapes=[pltpu.VMEM((f, tn), F32), pltpu.VMEM((f, tn), BF16),
                            pltpu.SemaphoreType.DMA((1,))]),
        out_shape=jax.ShapeDtypeStruct((npad, d // 2), jnp.uint32),
        name="expert_down",
        compiler_params=pltpu.CompilerParams(
            dimension_semantics=("arbitrary", "arbitrary"),
            vmem_limit_bytes=_vmem_limit(f * tn * 4 + f * tn * 2 + 2 * tm * f * 2 + 6 * tm * tn * 4)),
    )(*sched, hg, w_down)


def _combine_ln_kernel(pos_ref, y_hbm, x_ref, w0_ref, w1_ref, g_ref, b_ref, o_ref, ob_ref, buf, sem,
                       *, alpha, m, tn):
    i = pl.program_id(0)
    tc, d = x_ref.shape

    def row_copy(slot, s, r, p):
        return pltpu.make_async_copy(y_hbm.at[pl.ds(p, 1), :], buf.at[slot, s, pl.ds(r, 1), :], sem.at[slot])

    def gather(step):
        def issue(r, carry):
            row_copy(step % 2, 0, r, pos_ref[step * tc + r]).start()
            row_copy(step % 2, 1, r, pos_ref[m + step * tc + r]).start()
            return carry
        lax.fori_loop(0, tc, issue, 0, unroll=8)

    @pl.when(i == 0)
    def _():
        gather(i)

    @pl.when(i + 1 < pl.num_programs(0))
    def _():
        gather(i + 1)

    def drain(r, carry):
        row_copy(i % 2, 0, r, 0).wait()
        row_copy(i % 2, 1, r, 0).wait()
        return carry
    lax.fori_loop(0, tc, drain, 0, unroll=8)

    w0, w1 = w0_ref[...], w1_ref[...]
    rows = buf.at[i % 2]
    parts = []
    for j in range(d // tn):
        cs = pl.ds(j * (tn // 2), tn // 2)
        hi0, lo0 = _unpack_halves(rows[0, :, cs])
        hi1, lo1 = _unpack_halves(rows[1, :, cs])
        parts += [w0 * hi0 + w1 * hi1, w0 * lo0 + w1 * lo1]
    z = alpha * x_ref[...] + jnp.concatenate(parts, axis=1)
    mu = jnp.mean(z, axis=-1, keepdims=True)
    dz = z - mu
    var = jnp.mean(dz * dz, axis=-1, keepdims=True)
    x = dz * lax.rsqrt(var + LN_EPS) * g_ref[...] + b_ref[...]
    o_ref[...] = x
    ob_ref[...] = x.astype(BF16)


def _combine_ln(yg, pos, ewts, x, ln_g, ln_b, idx, *, alpha, tn):
    m, d = x.shape
    tc = _pick(m, 256, 8)
    row = lambda i, pos: (i, 0)
    par = lambda i, pos: (idx, 0, 0)
    return pl.pallas_call(
        functools.partial(_combine_ln_kernel, alpha=alpha, m=m, tn=tn),
        grid_spec=pltpu.PrefetchScalarGridSpec(
            num_scalar_prefetch=1, grid=(m // tc,),
            in_specs=[pl.BlockSpec(memory_space=pl.ANY),
                      pl.BlockSpec((tc, d), row),
                      pl.BlockSpec((None, tc, 1), lambda i, pos: (0, i, 0)),
                      pl.BlockSpec((None, tc, 1), lambda i, pos: (1, i, 0)),
                      pl.BlockSpec((None, 1, d), par), pl.BlockSpec((None, 1, d), par)],
            out_specs=[pl.BlockSpec((tc, d), row), pl.BlockSpec((tc, d), row)],
            scratch_shapes=[pltpu.VMEM((2, 2, tc, d // 2), jnp.uint32), pltpu.SemaphoreType.DMA((2,))]),
        out_shape=[jax.ShapeDtypeStruct((m, d), F32), jax.ShapeDtypeStruct((m, d), BF16)],
        name="combine_ln",
        compiler_params=pltpu.CompilerParams(
            dimension_semantics=("arbitrary",), vmem_limit_bytes=_vmem_limit(tc * d * 48)),
    )(pos, yg, x, ewts, ewts, ln_g, ln_b)


def _expert_schedule(eidx, rank, counts, tm, n_tiles_max):
    n_experts = counts.shape[0]
    tiles = (counts + tm - 1) // tm
    tile_end = jnp.cumsum(tiles)
    tile_start = tile_end - tiles
    experts = jnp.arange(n_experts, dtype=jnp.int32)
    pick = eidx[:, :, None] == experts
    pos = rank + jnp.sum(jnp.where(pick, tile_start * tm, 0), axis=-1)
    n_tiles = tile_end[-1]
    tile_ids = jnp.arange(n_tiles_max, dtype=jnp.int32)
    tile_e = jnp.sum(tile_end[None, :] <= jnp.minimum(tile_ids, n_tiles - 1)[:, None], axis=-1)
    mine = tile_e[:, None] == experts
    filled = jnp.sum(jnp.where(mine, counts - (tile_ids[:, None] - tile_start) * tm, 0), axis=-1)
    tile_rows = jnp.clip(filled, 0, tm)
    after = jnp.sum(jnp.where(mine, tile_end, 0), axis=-1)
    expert_after = jnp.sum(tile_end[None, :] <= after[:, None], axis=-1)
    next_e = jnp.where(after < n_tiles, expert_after, -1)
    i32 = lambda a: a.astype(jnp.int32)
    return i32(pos).reshape(-1), (i32(tile_e), i32(tile_rows), i32(n_tiles).reshape(1), i32(next_e))


def kernel(x, p, w_in, b_branch_gate, w_gla_gate_up, b_gla_gate, conv_w, conv_b, conv_ln_g, conv_ln_b, w_conv_out, gla_norm_g, w_gla_out, w_out, ln_g, ln_b, w_router, b_router, w_exp_gate, w_exp_up, w_exp_down, w_ple, w_ple_gate):
    bsz, seq, d = x.shape
    assert bsz == 1
    depth = w_in.shape[0]
    m = bsz * seq
    conv_ch = conv_w.shape[-1]
    rank, dk = w_gla_gate_up.shape[1], w_gla_gate_up.shape[2]
    dv = gla_norm_g.shape[-1]
    n_experts = w_router.shape[1]
    off_b = conv_ch
    off_q = 2 * conv_ch
    off_alpha = off_q + 2 * dk + 2 * dv
    off_gate = off_alpha + rank
    alpha = (2.0 * depth) ** 0.25
    tm_e = 512 if m >= 4096 else 64
    n_tiles_max = (2 * m) // tm_e + n_experts
    tn_down = _pick(d, 2048)

    xf = x.reshape(m, d)
    xb = xf.astype(BF16)
    pb = p.reshape(depth, m, p.shape[-1]).astype(BF16)
    w_in_t = jnp.swapaxes(w_in, 1, 2)
    b_gate = b_branch_gate.reshape(depth, 1, 2 * d)
    wup_pad = jnp.zeros((depth, LANES, dk), F32).at[:, :rank, :].set(w_gla_gate_up)
    b_a = b_gla_gate.reshape(depth, 1, dk)
    conv_b3 = conv_b.reshape(depth, 1, conv_ch)
    cln_g = conv_ln_g.reshape(depth, 1, conv_ch)
    cln_b = conv_ln_b.reshape(depth, 1, conv_ch)
    gnorm = gla_norm_g.reshape(depth, 1, dv)
    ln_g3 = ln_g.reshape(depth * 3, 1, d)
    ln_b3 = ln_b.reshape(depth * 3, 1, d)
    wr_pad = jnp.zeros((d, LANES), F32).at[:, :n_experts].set(w_router)
    br_pad = jnp.zeros((1, LANES), F32).at[0, :n_experts].set(b_router)

    tm = _pick(m, 1024, 8)
    for i in range(depth):
        (u,) = _fused_matmul(
            [xb], [(0, w_in_t, i, 0, True), (0, w_in_t, i, off_b, True)], [],
            lambda ds, ex: [ds[0] * jax.nn.sigmoid(ds[1])], [F32],
            n_cols=conv_ch, tm=tm, tn=_pick(conv_ch, 256), name="in_proj_glu")
        (h2,) = _fused_matmul(
            [xb], [(0, w_in_t, i, off_q, True)], [], lambda ds, ex: [ds[0]], [F32],
            n_cols=2 * dk + 2 * dv, tm=tm, tn=_pick(dk, 512), name="in_proj_qkvr")
        (a_low,) = _fused_matmul(
            [xb], [(0, w_in_t, i, off_alpha, True)], [], lambda ds, ex: [ds[0]], [F32],
            n_cols=LANES, tm=tm, tn=LANES, name="in_proj_alpha")
        (gates,) = _fused_matmul(
            [xb], [(0, w_in_t, i, off_gate, True)], [('row', b_gate, i, 0)],
            lambda ds, ex: [jax.nn.sigmoid(ds[0] + ex[0])], [BF16],
            n_cols=2 * d, tm=tm, tn=_pick(d, 512), name="in_proj_gates")

        uc = _conv_ln_silu(u, conv_w, conv_b3, cln_g, cln_b, i)
        og = _gla(h2, a_low, wup_pad, b_a, gnorm, i, dk=dk, dv=dv, rank=rank)

        tn_o = _pick(d, 512)
        (merged,) = _fused_matmul(
            [uc, og], [(0, w_conv_out, i, 0, False), (1, w_gla_out, i, 0, False)],
            [('tile', gates, None, 0), ('tile', gates, None, d)],
            lambda ds, ex: [ex[0].astype(F32) * ds[0] + ex[1].astype(F32) * ds[1]], [BF16],
            n_cols=d, tm=tm, tn=tn_o, name="branch_out_merge")
        (z1,) = _fused_matmul(
            [merged], [(0, w_out, i, 0, False)], [('tile', xf, None, 0)],
            lambda ds, ex: [alpha * ex[0] + ds[0]], [F32],
            n_cols=d, tm=tm, tn=tn_o, name="mixer_out")
        x1, x1p, logits = _layer_norm(z1, ln_g3, ln_b3, 3 * i, router=(wr_pad, br_pad))

        logits_t = logits[:, :n_experts].T.reshape(n_experts, m // LANES, LANES)
        eidx, ewts, pair_rank, counts = _route(logits_t)
        pos, sched = _expert_schedule(
            eidx.reshape(2, m), pair_rank.reshape(2, m), counts[:, 0], tm_e, n_tiles_max)
        xg = _dispatch_rows(x1p, pos, n_tiles_max * tm_e)
        hg = _expert_up(xg, w_exp_gate, w_exp_up, i, sched, tm=tm_e)
        yg = _expert_down(hg, w_exp_down, i, sched, tm=tm_e, tn=tn_down)
        x2, x2b = _combine_ln(yg, pos, ewts.reshape(2, m, 1), x1, ln_g3, ln_b3, 3 * i + 1,
                              alpha=alpha, tn=tn_down)

        (z3,) = _fused_matmul(
            [x2b, pb[i]], [(0, w_ple_gate, i, 0, False), (1, w_ple, i, 0, False)], [('tile', x2, None, 0)],
            lambda ds, ex: [alpha * ex[0] + jax.nn.sigmoid(ds[0]) * ds[1]], [F32],
            n_cols=d, tm=tm, tn=tn_o, name="ple")
        xf, xb = _layer_norm(z3, ln_g3, ln_b3, 3 * i + 2)

    return xf.reshape(bsz, seq, d)
```

```python
import functools

import jax
import jax.numpy as jnp
from jax import lax
from jax.experimental import pallas as pl
from jax.experimental.pallas import tpu as pltpu

F32 = jnp.float32
BF16 = jnp.bfloat16

GLA_HEADS = 4
GLA_GATE_TEMP = 16.0
GLA_CHUNK = 64
N_GROUPS = 4
LN_EPS = 1e-5
LANES = 128
V7X_SCOPED_VMEM_BYTES = 60000 * 1024


def _pick(n, pref, mult=LANES):
    if n <= pref:
        return n
    t = (pref // mult) * mult
    while t >= mult:
        if n % t == 0:
            return t
        t -= mult
    return n


def _vmem_limit(nbytes):
    return int(min(max(nbytes * 5 // 4 + (4 << 20), 16 << 20), V7X_SCOPED_VMEM_BYTES))


def _split3(x):
    hi = x.astype(BF16)
    r1 = x - hi.astype(F32)
    mid = r1.astype(BF16)
    lo = (r1 - mid.astype(F32)).astype(BF16)
    return hi, mid, lo


def _dot(a, b):
    return jnp.dot(a, b, preferred_element_type=F32)


def _sigmoid(x):
    return 0.5 * jnp.tanh(0.5 * x) + 0.5


def _dot_f32(a, b):
    ah, am, _ = _split3(a)
    bh, bm, _ = _split3(b)
    return _dot(ah, bh) + (_dot(ah, bm) + _dot(am, bh))


NT_DIMS = (((1,), (1,)), ((), ()))


def _fused_matmul_kernel(*refs, n_acts, dot_act, forms, n_extras, n_outs, epilogue):
    n_w = sum(2 if shift else 1 for _, shift in forms)
    act_refs = refs[:n_acts]
    w_refs = list(refs[n_acts:n_acts + n_w])
    ex_refs = refs[n_acts + n_w:n_acts + n_w + n_extras]
    out_refs = refs[n_acts + n_w + n_extras:n_acts + n_w + n_extras + n_outs]
    w_bf16 = refs[n_acts + n_w + n_extras + n_outs:]

    @pl.when(pl.program_id(1) == 0)
    def _():
        for s, (_, shift) in zip(w_bf16, forms):
            w_ref = w_refs.pop(0)
            if shift:
                tn = s.shape[0]
                s[pl.ds(0, tn - shift), :] = w_ref[pl.ds(shift, tn - shift), :].astype(BF16)
                s[pl.ds(tn - shift, shift), :] = w_refs.pop(0)[...].astype(BF16)
            else:
                s[...] = w_ref[...].astype(BF16)

    dots = []
    for ai, s, (output_major, _) in zip(dot_act, w_bf16, forms):
        a = act_refs[ai][...]
        dots.append(lax.dot_general(a, s[...], NT_DIMS, preferred_element_type=F32) if output_major
                    else _dot(a, s[...]))
    outs = epilogue(dots, [e[...] for e in ex_refs])
    for o_ref, o in zip(out_refs, outs):
        o_ref[...] = o.astype(o_ref.dtype)


def _fused_matmul(acts, dots, extras, epilogue, out_dtypes, *, n_cols, tm, tn, name):
    m = acts[0].shape[0]
    assert m % tm == 0 and n_cols % tn == 0
    grid = (n_cols // tn, m // tm)
    in_specs, args, scratch, forms = [], [], [], []
    nbytes = 0
    for a in acts:
        k = a.shape[1]
        in_specs.append(pl.BlockSpec((tm, k), lambda j, i: (i, 0)))
        args.append(a)
        nbytes += 2 * tm * k * a.dtype.itemsize
    for (_, w, layer, off, output_major) in dots:
        shift = off % tn
        if output_major:
            k = w.shape[2]
            in_specs.append(pl.BlockSpec((None, tn, k), lambda j, i, layer=layer, cb=off // tn: (layer, cb + j, 0)))
            args.append(w)
            if shift:
                assert shift % 16 == 0 and tn % shift == 0 and (off - shift) % shift == 0
                in_specs.append(pl.BlockSpec(
                    (None, shift, k),
                    lambda j, i, layer=layer, cb=off // tn, r=tn // shift: (layer, (cb + j + 1) * r, 0)))
                args.append(w)
                nbytes += 2 * shift * k * 4
            scratch.append(pltpu.VMEM((tn, k), BF16))
        else:
            assert shift == 0
            k = w.shape[1]
            in_specs.append(pl.BlockSpec((None, k, tn), lambda j, i, layer=layer, cb=off // tn: (layer, 0, cb + j)))
            args.append(w)
            scratch.append(pltpu.VMEM((k, tn), BF16))
        forms.append((output_major, shift))
        nbytes += 2 * k * tn * 4 + k * tn * 2
    for (kind, arr, layer, off) in extras:
        assert off % tn == 0
        if kind == 'tile':
            in_specs.append(pl.BlockSpec((tm, tn), lambda j, i, cb=off // tn: (i, cb + j)))
            nbytes += 2 * tm * tn * arr.dtype.itemsize
        else:
            in_specs.append(pl.BlockSpec((None, 1, tn), lambda j, i, layer=layer, cb=off // tn: (layer, 0, cb + j)))
        args.append(arr)
    out_specs = [pl.BlockSpec((tm, tn), lambda j, i: (i, j)) for _ in out_dtypes]
    out_shape = [jax.ShapeDtypeStruct((m, n_cols), dt) for dt in out_dtypes]
    nbytes += sum(2 * tm * tn * jnp.dtype(dt).itemsize for dt in out_dtypes)
    nbytes += (len(dots) + 3) * tm * tn * 4
    kernel = functools.partial(
        _fused_matmul_kernel, n_acts=len(acts), dot_act=tuple(d[0] for d in dots), forms=tuple(forms),
        n_extras=len(extras), n_outs=len(out_dtypes), epilogue=epilogue)
    return pl.pallas_call(
        kernel, grid=grid, in_specs=in_specs, out_specs=out_specs, out_shape=out_shape,
        scratch_shapes=scratch, name=name,
        compiler_params=pltpu.CompilerParams(
            dimension_semantics=("arbitrary", "arbitrary"), vmem_limit_bytes=_vmem_limit(nbytes)),
    )(*args)


def _conv_kernel(u_ref, prev_ref, w_ref, b_ref, g_ref, beta_ref, o_ref, win, shifted, ybuf, *, kw, halo, rows):
    i = pl.program_id(0)
    j = pl.program_id(1)
    tt, cc = u_ref.shape
    win[pl.ds(0, halo), :] = jnp.where(i > 0, prev_ref[...], 0.0)
    win[pl.ds(halo, tt), :] = u_ref[...]
    span = shifted.shape[1]
    for s in range(1, 8):
        shifted[s - 1] = win[pl.ds(s, span), :]
    lead = halo - (kw - 1)
    for r0 in range(0, tt, rows):
        acc = jnp.broadcast_to(b_ref[...], (rows, cc))
        for t in range(kw):
            s = (lead + t) % 8
            start = r0 + lead + t - s
            src = win[pl.ds(start, rows), :] if s == 0 else shifted[s - 1, pl.ds(start, rows), :]
            acc = acc + src * w_ref[pl.ds(t, 1), :]
        ybuf[j, pl.ds(r0, rows), :] = acc

    @pl.when(j == pl.num_programs(1) - 1)
    def _():
        nc = ybuf.shape[0]
        c = nc * cc
        s = jnp.zeros((tt, 1), F32)
        for k in range(nc):
            s = s + jnp.sum(ybuf[k], axis=-1, keepdims=True)
        mu = s * (1.0 / c)
        v = jnp.zeros((tt, 1), F32)
        for k in range(nc):
            d = ybuf[k] - mu
            v = v + jnp.sum(d * d, axis=-1, keepdims=True)
        rstd = lax.rsqrt(v * (1.0 / c) + LN_EPS)
        for k in range(nc):
            cs = slice(k * cc, (k + 1) * cc)
            y = (ybuf[k] - mu) * rstd * g_ref[:, cs] + beta_ref[:, cs]
            o_ref[:, cs] = (y * _sigmoid(y)).astype(o_ref.dtype)


def _conv_ln_silu(u, conv_w, conv_b, ln_g, ln_b, layer):
    m, c = u.shape
    kw = conv_w.shape[1]
    halo = -(-(kw - 1) // 8) * 8
    tt = _pick(m, 512, halo)
    cc = _pick(c, 256)
    rows = 64 if tt % 64 == 0 else tt
    kernel = functools.partial(_conv_kernel, kw=kw, halo=halo, rows=rows)
    return pl.pallas_call(
        kernel, grid=(m // tt, c // cc),
        in_specs=[
            pl.BlockSpec((tt, cc), lambda i, j: (i, j)),
            pl.BlockSpec((halo, cc), lambda i, j: (jnp.maximum(i * (tt // halo) - 1, 0), j)),
            pl.BlockSpec((None, kw, cc), lambda i, j: (layer, 0, j)),
            pl.BlockSpec((None, 1, cc), lambda i, j: (layer, 0, j)),
            pl.BlockSpec((None, 1, c), lambda i, j: (layer, 0, 0)),
            pl.BlockSpec((None, 1, c), lambda i, j: (layer, 0, 0)),
        ],
        out_specs=pl.BlockSpec((tt, c), lambda i, j: (i, 0)),
        out_shape=jax.ShapeDtypeStruct((m, c), BF16),
        scratch_shapes=[pltpu.VMEM((tt + halo, cc), F32), pltpu.VMEM((7, tt + halo - 8, cc), F32),
                        pltpu.VMEM((c // cc, tt, cc), F32)],
        name="conv_ln_silu",
        compiler_params=pltpu.CompilerParams(
            dimension_semantics=("arbitrary", "arbitrary"),
            vmem_limit_bytes=_vmem_limit(3 * tt * c * 4 + 14 * (tt + halo) * cc * 4)),
    )(u, u, conv_w, conv_b, ln_g, ln_b)


def _gla_kernel(q_ref, k_ref, v_ref, r_ref, a_ref, wup_ref, ba_ref, g_ref, o_ref,
                state, qe_s, qi_s, ki_s, kd_s, dec_s, obuf, *, rank, chunk):
    tb, dk = q_ref.shape
    dv = v_ref.shape[1]
    hk, hv = dk // GLA_HEADS, dv // GLA_HEADS
    nchunks = tb // chunk

    @pl.when(pl.program_id(0) == 0)
    def _():
        state[...] = jnp.zeros_like(state)

    lane = lax.broadcasted_iota(jnp.int32, a_ref.shape, 1)
    a_low = jnp.where(lane < rank, a_ref[...], 0.0)
    z = _dot_f32(a_low, wup_ref[...]) + ba_ref[...]
    la = (jnp.minimum(z, 0.0) - jnp.log(1.0 + jnp.exp(-jnp.abs(z)))) * (1.0 / GLA_GATE_TEMP)
    la_hi, la_lo, _ = _split3(la)

    rr = lax.broadcasted_iota(jnp.int32, (tb, tb), 0)
    cc = lax.broadcasted_iota(jnp.int32, (tb, tb), 1)
    sh = chunk.bit_length() - 1
    assert chunk == 1 << sh
    same = (rr >> sh) == (cc >> sh)
    tril = jnp.where(same & (cc <= rr), 1.0, 0.0).astype(BF16)
    ones_blk = jnp.where(same, 1.0, 0.0).astype(BF16)
    b = _dot(tril, la_hi) + _dot(tril, la_lo)
    bl = _dot(ones_blk, la_hi) + _dot(ones_blk, la_lo)

    sr = lax.broadcasted_iota(jnp.int32, (tb, nchunks * LANES), 0)
    sc = lax.broadcasted_iota(jnp.int32, (tb, nchunks * LANES), 1)
    sel = jnp.where((sr >> sh) == (sc >> (LANES.bit_length() - 1)), 1.0, 0.0).astype(BF16)
    tn_dims = (((0,), (0,)), ((), ()))
    tot_t = (lax.dot_general(la_hi, sel, tn_dims, preferred_element_type=F32)
             + lax.dot_general(la_lo, sel, tn_dims, preferred_element_type=F32))
    dec_s[...] = jnp.exp(tot_t)

    q = q_ref[...].astype(F32) * (float(hk) ** -0.5)
    k = k_ref[...].astype(F32)
    half = 0.5 * bl
    qe_s[...] = (q * jnp.exp(b)).astype(BF16)
    qi_s[...] = (q * jnp.exp(b - half)).astype(BF16)
    ki_s[...] = (k * jnp.exp(half - b)).astype(BF16)
    kd_s[...] = (k * jnp.exp(bl - b)).astype(BF16)

    ri = lax.broadcasted_iota(jnp.int32, (chunk, chunk), 0)
    ci = lax.broadcasted_iota(jnp.int32, (chunk, chunk), 1)
    causal = ci <= ri
    nt_dims = (((1,), (1,)), ((), ()))
    for c in range(nchunks):
        rows = pl.ds(c * chunk, chunk)
        for h in range(GLA_HEADS):
            ks = pl.ds(h * hk, hk)
            vs = pl.ds(h * hv, hv)
            vch = v_ref[rows, vs].astype(BF16)
            scores = lax.dot_general(qi_s[rows, ks], ki_s[rows, ks], nt_dims, preferred_element_type=F32)
            scores = jnp.where(causal, scores, 0.0).astype(BF16)
            s_old = state[h]
            o = _dot(qe_s[rows, ks], s_old.astype(BF16)) + _dot(scores, vch)
            obuf[rows, vs] = o
            dcol = dec_s[ks, pl.ds(c * LANES, LANES)]
            decay = jnp.concatenate([dcol] * (hv // LANES), axis=1) if hv >= LANES else dcol[:, :hv]
            state[h] = s_old * decay + lax.dot_general(kd_s[rows, ks], vch, tn_dims, preferred_element_type=F32)

    for h in range(GLA_HEADS):
        vs = pl.ds(h * hv, hv)
        o = obuf[:, vs]
        ms = jnp.mean(o * o, axis=-1, keepdims=True)
        on = o * lax.rsqrt(ms + LN_EPS) * g_ref[:, vs]
        r = r_ref[:, vs].astype(F32)
        o_ref[:, vs] = (on * (r * _sigmoid(r))).astype(o_ref.dtype)


def _gla(h2, a_low, wup_pad, b_a, norm_g, layer, *, dk, dv, rank):
    m = h2.shape[0]
    tb = _pick(m, 256, GLA_CHUNK)
    assert tb % GLA_CHUNK == 0 and dv == 2 * dk
    nchunks = tb // GLA_CHUNK
    hk, hv = dk // GLA_HEADS, dv // GLA_HEADS
    kernel = functools.partial(_gla_kernel, rank=rank, chunk=GLA_CHUNK)
    return pl.pallas_call(
        kernel, grid=(m // tb,),
        in_specs=[
            pl.BlockSpec((tb, dk), lambda i: (i, 0)),
            pl.BlockSpec((tb, dk), lambda i: (i, 1)),
            pl.BlockSpec((tb, dv), lambda i: (i, 1)),
            pl.BlockSpec((tb, dv), lambda i: (i, 2)),
            pl.BlockSpec((tb, LANES), lambda i: (i, 0)),
            pl.BlockSpec((None, LANES, dk), lambda i: (layer, 0, 0)),
            pl.BlockSpec((None, 1, dk), lambda i: (layer, 0, 0)),
            pl.BlockSpec((None, 1, dv), lambda i: (layer, 0, 0)),
        ],
        out_specs=pl.BlockSpec((tb, dv), lambda i: (i, 0)),
        out_shape=jax.ShapeDtypeStruct((m, dv), BF16),
        scratch_shapes=[
            pltpu.VMEM((GLA_HEADS, hk, hv), F32),
            pltpu.VMEM((tb, dk), BF16), pltpu.VMEM((tb, dk), BF16),
            pltpu.VMEM((tb, dk), BF16), pltpu.VMEM((tb, dk), BF16),
            pltpu.VMEM((dk, nchunks * LANES), F32),
            pltpu.VMEM((tb, dv), F32),
        ],
        name="gla",
        compiler_params=pltpu.CompilerParams(
            dimension_semantics=("arbitrary",), vmem_limit_bytes=_vmem_limit(48 << 20)),
    )(h2, h2, h2, h2, a_low, wup_pad, b_a, norm_g)


def _pack_halves(x):
    n = x.shape[1] // 2
    hi = lax.bitcast_convert_type(x[:, :n].astype(BF16).astype(F32), jnp.uint32)
    lo = lax.bitcast_convert_type(x[:, n:].astype(BF16).astype(F32), jnp.uint32)
    return hi | (lo >> 16)


def _unpack_halves(w):
    hi = lax.bitcast_convert_type(w & jnp.uint32(0xFFFF0000), F32)
    lo = lax.bitcast_convert_type(w << 16, F32)
    return hi, lo


def _ln_kernel(z_ref, g_ref, b_ref, *rest, with_router):
    if with_router:
        wr_ref, br_ref, x_ref, xp_ref, lg_ref = rest
    else:
        x_ref, *xb_refs = rest
    z = z_ref[...]
    mu = jnp.mean(z, axis=-1, keepdims=True)
    d = z - mu
    var = jnp.mean(d * d, axis=-1, keepdims=True)
    x = d * lax.rsqrt(var + LN_EPS) * g_ref[...] + b_ref[...]
    x_ref[...] = x
    if with_router:
        xp_ref[...] = _pack_halves(x)
        lg_ref[...] = _dot_f32(x, wr_ref[...]) + br_ref[...]
    for xb_ref in ([] if with_router else xb_refs):
        xb_ref[...] = x.astype(BF16)


def _layer_norm(z, ln_g, ln_b, idx, router=None, bf16_copy=True):
    m, d = z.shape
    tr = _pick(m, 512, 8)
    in_specs = [
        pl.BlockSpec((tr, d), lambda i: (i, 0)),
        pl.BlockSpec((None, 1, d), lambda i: (idx, 0, 0)),
        pl.BlockSpec((None, 1, d), lambda i: (idx, 0, 0)),
    ]
    args = [z, ln_g, ln_b]
    if router is None:
        out_specs = [pl.BlockSpec((tr, d), lambda i: (i, 0))] * (2 if bf16_copy else 1)
        out_shape = [jax.ShapeDtypeStruct((m, d), F32), jax.ShapeDtypeStruct((m, d), BF16)][:len(out_specs)]
    else:
        in_specs += [pl.BlockSpec((d, LANES), lambda i: (0, 0)), pl.BlockSpec((1, LANES), lambda i: (0, 0))]
        args += list(router)
        out_specs = [pl.BlockSpec((tr, d), lambda i: (i, 0)), pl.BlockSpec((tr, d // 2), lambda i: (i, 0)),
                     pl.BlockSpec((tr, LANES), lambda i: (i, 0))]
        out_shape = [jax.ShapeDtypeStruct((m, d), F32), jax.ShapeDtypeStruct((m, d // 2), jnp.uint32),
                     jax.ShapeDtypeStruct((m, LANES), F32)]
    return pl.pallas_call(
        functools.partial(_ln_kernel, with_router=router is not None),
        grid=(m // tr,), in_specs=in_specs, out_specs=out_specs, out_shape=out_shape,
        name="layer_norm_router" if router is not None else "layer_norm",
        compiler_params=pltpu.CompilerParams(
            dimension_semantics=("arbitrary",), vmem_limit_bytes=_vmem_limit(tr * d * 36)),
    )(*args)


def _route_kernel(lg_ref, idx_ref, w_ref, rank_ref, cnt_ref, *, n_experts):
    per = n_experts // N_GROUPS
    logit = [lg_ref[e] for e in range(n_experts)]
    mx = functools.reduce(jnp.maximum, logit)
    ex = [jnp.exp(l - mx) for l in logit]
    tot = functools.reduce(lambda a, c: a + c, ex)
    p = [e / tot for e in ex]

    def top2_sum(vals):
        a, bb, c, d = vals
        hi1, lo1 = jnp.maximum(a, bb), jnp.minimum(a, bb)
        hi2, lo2 = jnp.maximum(c, d), jnp.minimum(c, d)
        first = jnp.maximum(hi1, hi2)
        second = jnp.maximum(jnp.minimum(hi1, hi2), jnp.maximum(lo1, lo2))
        return first + second

    assert per == 4
    gs = [top2_sum(p[g * per:(g + 1) * per]) for g in range(N_GROUPS)]
    best = jnp.zeros_like(gs[0], dtype=jnp.int32)
    best_v = gs[0]
    for g in range(1, N_GROUPS):
        upd = gs[g] > best_v
        best = jnp.where(upd, g, best)
        best_v = jnp.where(upd, gs[g], best_v)
    vals = []
    for k in range(per):
        v = p[k]
        for g in range(1, N_GROUPS):
            v = jnp.where(best == g, p[g * per + k], v)
        vals.append(v)
    i1 = jnp.zeros_like(best)
    v1 = vals[0]
    for k in range(1, per):
        upd = vals[k] > v1
        i1 = jnp.where(upd, k, i1)
        v1 = jnp.where(upd, vals[k], v1)
    i2 = jnp.full_like(best, -1)
    v2 = jnp.full_like(v1, -1.0)
    for k in range(per):
        upd = (i1 != k) & (vals[k] > v2)
        i2 = jnp.where(upd, k, i2)
        v2 = jnp.where(upd, vals[k], v2)
    den = v1 + v2
    picks = (best * per + i1, best * per + i2)
    idx_ref[0] = picks[0]
    idx_ref[1] = picks[1]
    w_ref[0] = v1 / den
    w_ref[1] = v2 / den

    rows, lanes = best.shape
    li = lax.broadcasted_iota(jnp.int32, (lanes, lanes), 0)
    lj = lax.broadcasted_iota(jnp.int32, (lanes, lanes), 1)
    upper = jnp.where(li <= lj, 1.0, 0.0).astype(BF16)
    ones = jnp.ones((lanes, lanes), BF16)
    ri = lax.broadcasted_iota(jnp.int32, (rows, rows), 0)
    rj = lax.broadcasted_iota(jnp.int32, (rows, rows), 1)
    before = jnp.where(rj < ri, 1.0, 0.0).astype(BF16)
    ranks = [jnp.zeros((rows, lanes), F32), jnp.zeros((rows, lanes), F32)]
    for e in range(n_experts):
        base = jnp.zeros((1, lanes), F32)
        for s in range(2):
            hit = picks[s] == e
            ind = jnp.where(hit, 1.0, 0.0).astype(BF16)
            in_row = _dot(ind, upper)
            row_tot = _dot(ind, ones)
            row_off = _dot(before, row_tot.astype(BF16))
            ranks[s] = jnp.where(hit, in_row + row_off + (base - 1.0), ranks[s])
            base = base + (row_off + row_tot)[rows - 1:rows, :]
        cnt_ref[pl.ds(e, 1), :] = base.astype(jnp.int32)
    rank_ref[0] = ranks[0].astype(jnp.int32)
    rank_ref[1] = ranks[1].astype(jnp.int32)


def _route(logits_t):
    n_experts, rows, lanes = logits_t.shape
    return pl.pallas_call(
        functools.partial(_route_kernel, n_experts=n_experts),
        out_shape=[jax.ShapeDtypeStruct((2, rows, lanes), jnp.int32),
                   jax.ShapeDtypeStruct((2, rows, lanes), F32),
                   jax.ShapeDtypeStruct((2, rows, lanes), jnp.int32),
                   jax.ShapeDtypeStruct((n_experts, lanes), jnp.int32)],
        name="route",
    )(logits_t)


def _dispatch_kernel(pos_ref, x_ref, init_hbm, o_hbm, sem, *, m):
    del init_hbm
    i = pl.program_id(0)
    chunk = x_ref.shape[0]

    def row_copy(r, dst):
        return pltpu.make_async_copy(x_ref.at[pl.ds(r, 1), :], o_hbm.at[pl.ds(dst, 1), :], sem)

    def issue(r, carry):
        row_copy(r, pos_ref[i * chunk + r]).start()
        row_copy(r, pos_ref[m + i * chunk + r]).start()
        return carry
    lax.fori_loop(0, chunk, issue, 0, unroll=8)

    def drain(r, carry):
        row_copy(r, 0).wait()
        row_copy(r, 0).wait()
        return carry
    lax.fori_loop(0, chunk, drain, 0, unroll=8)


def _dispatch_rows(xp, pos, npad):
    m, n = xp.shape
    chunk = _pick(m, 512, 8)
    return pl.pallas_call(
        functools.partial(_dispatch_kernel, m=m),
        grid_spec=pltpu.PrefetchScalarGridSpec(
            num_scalar_prefetch=1, grid=(m // chunk,),
            in_specs=[pl.BlockSpec((chunk, n), lambda i, pos: (i, 0)), pl.BlockSpec(memory_space=pl.ANY)],
            out_specs=pl.BlockSpec(memory_space=pl.ANY),
            scratch_shapes=[pltpu.SemaphoreType.DMA(())]),
        out_shape=jax.ShapeDtypeStruct((npad, n), xp.dtype),
        input_output_aliases={2: 0},
        name="expert_dispatch",
        compiler_params=pltpu.CompilerParams(dimension_semantics=("arbitrary",)),
    )(pos, xp, jnp.zeros((npad, n), xp.dtype))


def _stream_expert_weights(te_ref, nx_ref, live, w_hbm, stage, w_bf16, sem, layer):
    j, t = pl.program_id(0), pl.program_id(1)
    n_col_tiles = pl.num_programs(0)

    def fetch(k, e, jj):
        tw = stage[k].shape[1]
        src = w_hbm[k].at[layer, e, :, pl.ds(pl.multiple_of(jj * tw, tw), tw)]
        return pltpu.make_async_copy(src, stage[k], sem.at[k])

    @pl.when((j == 0) & (t == 0))
    def _():
        for k in range(len(stage)):
            fetch(k, te_ref[0], 0).start()

    @pl.when(live & ((t == 0) | (te_ref[t] != te_ref[jnp.maximum(t - 1, 0)])))
    def _():
        for k in range(len(stage)):
            fetch(k, te_ref[t], j).wait()
        rows = stage[0].shape[0]
        step = _pick(rows, 512, 16)

        def convert(c, carry):
            rs = pl.ds(pl.multiple_of(c * step, step), step)
            for k in range(len(stage)):
                w_bf16[k][rs, :] = stage[k][rs, :].astype(BF16)
            return carry
        lax.fori_loop(0, rows // step, convert, 0)
        nxt = nx_ref[t]

        @pl.when(nxt >= 0)
        def _():
            for k in range(len(stage)):
                fetch(k, nxt, j).start()

        @pl.when((nxt < 0) & (j + 1 < n_col_tiles))
        def _():
            for k in range(len(stage)):
                fetch(k, te_ref[0], j + 1).start()


ROW_PARTS = 4


def _compute_valid_rows(live, n_valid, compute, o_ref):
    tm = o_ref.shape[0]
    part = tm // ROW_PARTS
    for nr in range(part, tm + 1, part):
        @pl.when(live & (n_valid > nr - part) & (n_valid <= nr))
        def _(nr=nr):
            compute(nr)
            if nr < tm:
                o_ref[pl.ds(nr, tm - nr), :] = jnp.zeros((tm - nr, o_ref.shape[1]), o_ref.dtype)

    @pl.when(jnp.logical_not(live))
    def _():
        o_ref[...] = jnp.zeros_like(o_ref)


def _expert_up_kernel(te_ref, tr_ref, nt_ref, nx_ref, x_ref, wg_hbm, wu_hbm, o_ref,
                      wg_st, wu_st, wg_s, wu_s, sem, *, layer):
    t = pl.program_id(1)
    live = t < nt_ref[0]
    n_valid = tr_ref[t]
    half_d = x_ref.shape[1]
    _stream_expert_weights(te_ref, nx_ref, live, (wg_hbm, wu_hbm), (wg_st, wu_st), (wg_s, wu_s), sem, layer)

    def compute(nr):
        x_hi, x_lo = _unpack_halves(x_ref[pl.ds(0, nr), :])
        x_hi, x_lo = x_hi.astype(BF16), x_lo.astype(BF16)
        top, bot = pl.ds(0, half_d), pl.ds(half_d, half_d)
        g = _dot(x_hi, wg_s[top, :]) + _dot(x_lo, wg_s[bot, :])
        u = _dot(x_hi, wu_s[top, :]) + _dot(x_lo, wu_s[bot, :])
        o_ref[pl.ds(0, nr), :] = (g * _sigmoid(g) * u).astype(o_ref.dtype)

    _compute_valid_rows(live, n_valid, compute, o_ref)


def _live_tile(j, t, te, tr, nt, nx):
    return (jnp.minimum(t, nt[0] - 1), 0)


def _expert_up(xg, w_gate, w_up, layer, sched, *, tm):
    npad, half_d = xg.shape
    d = 2 * half_d
    f = w_gate.shape[-1]
    tf = _pick(f, 512)
    return pl.pallas_call(
        functools.partial(_expert_up_kernel, layer=layer),
        grid_spec=pltpu.PrefetchScalarGridSpec(
            num_scalar_prefetch=4, grid=(f // tf, npad // tm),
            in_specs=[pl.BlockSpec((tm, half_d), _live_tile),
                      pl.BlockSpec(memory_space=pl.ANY), pl.BlockSpec(memory_space=pl.ANY)],
            out_specs=pl.BlockSpec((tm, tf), lambda j, t, te, tr, nt, nx: (t, j)),
            scratch_shapes=[pltpu.VMEM((d, tf), F32), pltpu.VMEM((d, tf), F32),
                            pltpu.VMEM((d, tf), BF16), pltpu.VMEM((d, tf), BF16),
                            pltpu.SemaphoreType.DMA((2,))]),
        out_shape=jax.ShapeDtypeStruct((npad, f), BF16),
        name="expert_up",
        compiler_params=pltpu.CompilerParams(
            dimension_semantics=("arbitrary", "arbitrary"),
            vmem_limit_bytes=_vmem_limit(2 * d * tf * 4 + 2 * d * tf * 2 + 5 * tm * d * 2 + 8 * tm * tf * 4)),
    )(*sched, xg, w_gate, w_up)


def _expert_down_kernel(te_ref, tr_ref, nt_ref, nx_ref, h_ref, wd_hbm, o_ref, wd_st, wd_s, sem, *, layer):
    t = pl.program_id(1)
    live = t < nt_ref[0]
    n_valid = tr_ref[t]
    _stream_expert_weights(te_ref, nx_ref, live, (wd_hbm,), (wd_st,), (wd_s,), sem, layer)

    def compute(nr):
        o_ref[pl.ds(0, nr), :] = _pack_halves(_dot(h_ref[pl.ds(0, nr), :], wd_s[...]))

    _compute_valid_rows(live, n_valid, compute, o_ref)


def _expert_down(hg, w_down, layer, sched, *, tm, tn):
    npad, f = hg.shape
    d = w_down.shape[-1]
    return pl.pallas_call(
        functools.partial(_expert_down_kernel, layer=layer),
        grid_spec=pltpu.PrefetchScalarGridSpec(
            num_scalar_prefetch=4, grid=(d // tn, npad // tm),
            in_specs=[pl.BlockSpec((tm, f), _live_tile), pl.BlockSpec(memory_space=pl.ANY)],
            out_specs=pl.BlockSpec((tm, tn // 2), lambda j, t, te, tr, nt, nx: (t, j)),
            scratch_shapes=[pltpu.VMEM((f, tn), F32), pltpu.VMEM((f, tn), BF16),
                            pltpu.SemaphoreType.DMA((1,))]),
        out_shape=jax.ShapeDtypeStruct((npad, d // 2), jnp.uint32),
        name="expert_down",
        compiler_params=pltpu.CompilerParams(
            dimension_semantics=("arbitrary", "arbitrary"),
            vmem_limit_bytes=_vmem_limit(f * tn * 4 + f * tn * 2 + 2 * tm * f * 2 + 6 * tm * tn * 4)),
    )(*sched, hg, w_down)


def _combine_ln_kernel(pos_ref, y_hbm, x_ref, w0_ref, w1_ref, g_ref, b_ref, o_ref, ob_ref, buf, sem,
                       *, alpha, m, tn):
    i = pl.program_id(0)
    tc, d = x_ref.shape

    def row_copy(slot, s, r, p):
        return pltpu.make_async_copy(y_hbm.at[pl.ds(p, 1), :], buf.at[slot, s, pl.ds(r, 1), :], sem.at[slot])

    def gather(step):
        def issue(r, carry):
            row_copy(step % 2, 0, r, pos_ref[step * tc + r]).start()
            row_copy(step % 2, 1, r, pos_ref[m + step * tc + r]).start()
            return carry
        lax.fori_loop(0, tc, issue, 0, unroll=8)

    @pl.when(i == 0)
    def _():
        gather(i)

    @pl.when(i + 1 < pl.num_programs(0))
    def _():
        gather(i + 1)

    def drain(r, carry):
        row_copy(i % 2, 0, r, 0).wait()
        row_copy(i % 2, 1, r, 0).wait()
        return carry
    lax.fori_loop(0, tc, drain, 0, unroll=8)

    w0, w1 = w0_ref[...], w1_ref[...]
    rows = buf.at[i % 2]
    parts = []
    for j in range(d // tn):
        cs = pl.ds(j * (tn // 2), tn // 2)
        hi0, lo0 = _unpack_halves(rows[0, :, cs])
        hi1, lo1 = _unpack_halves(rows[1, :, cs])
        parts += [w0 * hi0 + w1 * hi1, w0 * lo0 + w1 * lo1]
    z = alpha * x_ref[...] + jnp.concatenate(parts, axis=1)
    mu = jnp.mean(z, axis=-1, keepdims=True)
    dz = z - mu
    var = jnp.mean(dz * dz, axis=-1, keepdims=True)
    x = dz * lax.rsqrt(var + LN_EPS) * g_ref[...] + b_ref[...]
    o_ref[...] = x
    ob_ref[...] = x.astype(BF16)


def _combine_ln(yg, pos, ewts, x, ln_g, ln_b, idx, *, alpha, tn):
    m, d = x.shape
    tc = _pick(m, 256, 8)
    row = lambda i, pos: (i, 0)
    par = lambda i, pos: (idx, 0, 0)
    return pl.pallas_call(
        functools.partial(_combine_ln_kernel, alpha=alpha, m=m, tn=tn),
        grid_spec=pltpu.PrefetchScalarGridSpec(
            num_scalar_prefetch=1, grid=(m // tc,),
            in_specs=[pl.BlockSpec(memory_space=pl.ANY),
                      pl.BlockSpec((tc, d), row),
                      pl.BlockSpec((None, tc, 1), lambda i, pos: (0, i, 0)),
                      pl.BlockSpec((None, tc, 1), lambda i, pos: (1, i, 0)),
                      pl.BlockSpec((None, 1, d), par), pl.BlockSpec((None, 1, d), par)],
            out_specs=[pl.BlockSpec((tc, d), row), pl.BlockSpec((tc, d), row)],
            scratch_shapes=[pltpu.VMEM((2, 2, tc, d // 2), jnp.uint32), pltpu.SemaphoreType.DMA((2,))]),
        out_shape=[jax.ShapeDtypeStruct((m, d), F32), jax.ShapeDtypeStruct((m, d), BF16)],
        name="combine_ln",
        compiler_params=pltpu.CompilerParams(
            dimension_semantics=("arbitrary",), vmem_limit_bytes=_vmem_limit(tc * d * 48)),
    )(pos, yg, x, ewts, ewts, ln_g, ln_b)


def _expert_schedule(eidx, rank, counts, tm, n_tiles_max):
    n_experts = counts.shape[0]
    tiles = (counts + tm - 1) // tm
    tile_end = jnp.cumsum(tiles)
    tile_start = tile_end - tiles
    experts = jnp.arange(n_experts, dtype=jnp.int32)
    pick = eidx[:, :, None] == experts
    pos = rank + jnp.sum(jnp.where(pick, tile_start * tm, 0), axis=-1)
    n_tiles = tile_end[-1]
    tile_ids = jnp.arange(n_tiles_max, dtype=jnp.int32)
    tile_e = jnp.sum(tile_end[None, :] <= jnp.minimum(tile_ids, n_tiles - 1)[:, None], axis=-1)
    mine = tile_e[:, None] == experts
    filled = jnp.sum(jnp.where(mine, counts - (tile_ids[:, None] - tile_start) * tm, 0), axis=-1)
    tile_rows = jnp.clip(filled, 0, tm)
    after = jnp.sum(jnp.where(mine, tile_end, 0), axis=-1)
    expert_after = jnp.sum(tile_end[None, :] <= after[:, None], axis=-1)
    next_e = jnp.where(after < n_tiles, expert_after, -1)
    i32 = lambda a: a.astype(jnp.int32)
    return i32(pos).reshape(-1), (i32(tile_e), i32(tile_rows), i32(n_tiles).reshape(1), i32(next_e))


def kernel(x, p, w_in, b_branch_gate, w_gla_gate_up, b_gla_gate, conv_w, conv_b, conv_ln_g, conv_ln_b, w_conv_out, gla_norm_g, w_gla_out, w_out, ln_g, ln_b, w_router, b_router, w_exp_gate, w_exp_up, w_exp_down, w_ple, w_ple_gate):
    bsz, seq, d = x.shape
    assert bsz == 1
    depth = w_in.shape[0]
    m = bsz * seq
    conv_ch = conv_w.shape[-1]
    rank, dk = w_gla_gate_up.shape[1], w_gla_gate_up.shape[2]
    dv = gla_norm_g.shape[-1]
    n_experts = w_router.shape[1]
    off_b = conv_ch
    off_q = 2 * conv_ch
    off_alpha = off_q + 2 * dk + 2 * dv
    off_gate = off_alpha + rank
    alpha = (2.0 * depth) ** 0.25
    tm_e = 512 if m >= 4096 else 64
    n_tiles_max = (2 * m) // tm_e + n_experts
    tn_down = _pick(d, 2048)

    xf = x.reshape(m, d)
    xb = xf.astype(BF16)
    pb = p.reshape(depth, m, p.shape[-1]).astype(BF16)
    w_in_t = jnp.swapaxes(w_in, 1, 2)
    b_gate = b_branch_gate.reshape(depth, 1, 2 * d)
    wup_pad = jnp.zeros((depth, LANES, dk), F32).at[:, :rank, :].set(w_gla_gate_up)
    b_a = b_gla_gate.reshape(depth, 1, dk)
    conv_b3 = conv_b.reshape(depth, 1, conv_ch)
    cln_g = conv_ln_g.reshape(depth, 1, conv_ch)
    cln_b = conv_ln_b.reshape(depth, 1, conv_ch)
    gnorm = gla_norm_g.reshape(depth, 1, dv)
    ln_g3 = ln_g.reshape(depth * 3, 1, d)
    ln_b3 = ln_b.reshape(depth * 3, 1, d)
    wr_pad = jnp.zeros((d, LANES), F32).at[:, :n_experts].set(w_router)
    br_pad = jnp.zeros((1, LANES), F32).at[0, :n_experts].set(b_router)

    tm = _pick(m, 1024, 8)
    for i in range(depth):
        (u,) = _fused_matmul(
            [xb], [(0, w_in_t, i, 0, True), (0, w_in_t, i, off_b, True)], [],
            lambda ds, ex: [ds[0] * _sigmoid(ds[1])], [F32],
            n_cols=conv_ch, tm=tm, tn=_pick(conv_ch, 256), name="in_proj_glu")
        (h2,) = _fused_matmul(
            [xb], [(0, w_in_t, i, off_q, True)], [], lambda ds, ex: [ds[0]], [F32],
            n_cols=2 * dk + 2 * dv, tm=tm, tn=_pick(dk, 512), name="in_proj_qkvr")
        (a_low,) = _fused_matmul(
            [xb], [(0, w_in_t, i, off_alpha, True)], [], lambda ds, ex: [ds[0]], [F32],
            n_cols=LANES, tm=tm, tn=LANES, name="in_proj_alpha")
        (gates,) = _fused_matmul(
            [xb], [(0, w_in_t, i, off_gate, True)], [('row', b_gate, i, 0)],
            lambda ds, ex: [_sigmoid(ds[0] + ex[0])], [BF16],
            n_cols=2 * d, tm=tm, tn=_pick(d, 512), name="in_proj_gates")

        uc = _conv_ln_silu(u, conv_w, conv_b3, cln_g, cln_b, i)
        og = _gla(h2, a_low, wup_pad, b_a, gnorm, i, dk=dk, dv=dv, rank=rank)

        tn_o = _pick(d, 512)
        (merged,) = _fused_matmul(
            [uc, og], [(0, w_conv_out, i, 0, False), (1, w_gla_out, i, 0, False)],
            [('tile', gates, None, 0), ('tile', gates, None, d)],
            lambda ds, ex: [ex[0].astype(F32) * ds[0] + ex[1].astype(F32) * ds[1]], [BF16],
            n_cols=d, tm=tm, tn=tn_o, name="branch_out_merge")
        (z1,) = _fused_matmul(
            [merged], [(0, w_out, i, 0, False)], [('tile', xf, None, 0)],
            lambda ds, ex: [alpha * ex[0] + ds[0]], [F32],
            n_cols=d, tm=tm, tn=tn_o, name="mixer_out")
        x1, x1p, logits = _layer_norm(z1, ln_g3, ln_b3, 3 * i, router=(wr_pad, br_pad))

        logits_t = logits[:, :n_experts].T.reshape(n_experts, m // LANES, LANES)
        eidx, ewts, pair_rank, counts = _route(logits_t)
        pos, sched = _expert_schedule(
            eidx.reshape(2, m), pair_rank.reshape(2, m), counts[:, 0], tm_e, n_tiles_max)
        xg = _dispatch_rows(x1p, pos, n_tiles_max * tm_e)
        hg = _expert_up(xg, w_exp_gate, w_exp_up, i, sched, tm=tm_e)
        yg = _expert_down(hg, w_exp_down, i, sched, tm=tm_e, tn=tn_down)
        x2, x2b = _combine_ln(yg, pos, ewts.reshape(2, m, 1), x1, ln_g3, ln_b3, 3 * i + 1,
                              alpha=alpha, tn=tn_down)

        (z3,) = _fused_matmul(
            [x2b, pb[i]], [(0, w_ple_gate, i, 0, False), (1, w_ple, i, 0, False)], [('tile', x2, None, 0)],
            lambda ds, ex: [alpha * ex[0] + _sigmoid(ds[0]) * ds[1]], [F32],
            n_cols=d, tm=tm, tn=tn_o, name="ple")
        xf, *xb = _layer_norm(z3, ln_g3, ln_b3, 3 * i + 2, bf16_copy=i + 1 < depth)
        xb = xb[0] if xb else None

    return xf.reshape(bsz, seq, d)
```

```python
import functools

import jax
import jax.numpy as jnp
from jax import lax
from jax.experimental import pallas as pl
from jax.experimental.pallas import tpu as pltpu

F32 = jnp.float32
BF16 = jnp.bfloat16

GLA_HEADS = 4
GLA_GATE_TEMP = 16.0
GLA_CHUNK = 64
N_GROUPS = 4
LN_EPS = 1e-5
LANES = 128
V7X_SCOPED_VMEM_BYTES = 60000 * 1024


def _pick(n, pref, mult=LANES):
    if n <= pref:
        return n
    t = (pref // mult) * mult
    while t >= mult:
        if n % t == 0:
            return t
        t -= mult
    return n


def _vmem_limit(nbytes):
    return int(min(max(nbytes * 5 // 4 + (4 << 20), 16 << 20), V7X_SCOPED_VMEM_BYTES))


def _split3(x):
    hi = x.astype(BF16)
    r1 = x - hi.astype(F32)
    mid = r1.astype(BF16)
    lo = (r1 - mid.astype(F32)).astype(BF16)
    return hi, mid, lo


def _dot(a, b):
    return jnp.dot(a, b, preferred_element_type=F32)


def _sigmoid(x):
    return 0.5 * jnp.tanh(0.5 * x) + 0.5


def _dot_f32(a, b):
    ah, am, _ = _split3(a)
    bh, bm, _ = _split3(b)
    return _dot(ah, bh) + (_dot(ah, bm) + _dot(am, bh))


NT_DIMS = (((1,), (1,)), ((), ()))


def _fused_matmul_kernel(*refs, n_acts, dot_act, forms, n_extras, n_outs, epilogue):
    n_w = sum(2 if shift else 1 for _, shift in forms)
    act_refs = refs[:n_acts]
    w_refs = list(refs[n_acts:n_acts + n_w])
    ex_refs = refs[n_acts + n_w:n_acts + n_w + n_extras]
    out_refs = refs[n_acts + n_w + n_extras:n_acts + n_w + n_extras + n_outs]
    w_bf16 = refs[n_acts + n_w + n_extras + n_outs:]

    @pl.when(pl.program_id(1) == 0)
    def _():
        for s, (_, shift) in zip(w_bf16, forms):
            w_ref = w_refs.pop(0)
            if shift:
                tn = s.shape[0]
                s[pl.ds(0, tn - shift), :] = w_ref[pl.ds(shift, tn - shift), :].astype(BF16)
                s[pl.ds(tn - shift, shift), :] = w_refs.pop(0)[...].astype(BF16)
            else:
                s[...] = w_ref[...].astype(BF16)

    dots = []
    for ai, s, (output_major, _) in zip(dot_act, w_bf16, forms):
        a = act_refs[ai][...]
        dots.append(lax.dot_general(a, s[...], NT_DIMS, preferred_element_type=F32) if output_major
                    else _dot(a, s[...]))
    outs = epilogue(dots, [e[...] for e in ex_refs])
    for o_ref, o in zip(out_refs, outs):
        o_ref[...] = o.astype(o_ref.dtype)


def _fused_matmul(acts, dots, extras, epilogue, out_dtypes, *, n_cols, tm, tn, name):
    m = acts[0].shape[0]
    assert m % tm == 0 and n_cols % tn == 0
    grid = (n_cols // tn, m // tm)
    in_specs, args, scratch, forms = [], [], [], []
    nbytes = 0
    for a in acts:
        k = a.shape[1]
        in_specs.append(pl.BlockSpec((tm, k), lambda j, i: (i, 0)))
        args.append(a)
        nbytes += 2 * tm * k * a.dtype.itemsize
    for (_, w, layer, off, output_major) in dots:
        shift = off % tn
        if output_major:
            k = w.shape[2]
            in_specs.append(pl.BlockSpec((None, tn, k), lambda j, i, layer=layer, cb=off // tn: (layer, cb + j, 0)))
            args.append(w)
            if shift:
                assert shift % 16 == 0 and tn % shift == 0 and (off - shift) % shift == 0
                in_specs.append(pl.BlockSpec(
                    (None, shift, k),
                    lambda j, i, layer=layer, cb=off // tn, r=tn // shift: (layer, (cb + j + 1) * r, 0)))
                args.append(w)
                nbytes += 2 * shift * k * 4
            scratch.append(pltpu.VMEM((tn, k), BF16))
        else:
            assert shift == 0
            k = w.shape[1]
            in_specs.append(pl.BlockSpec((None, k, tn), lambda j, i, layer=layer, cb=off // tn: (layer, 0, cb + j)))
            args.append(w)
            scratch.append(pltpu.VMEM((k, tn), BF16))
        forms.append((output_major, shift))
        nbytes += 2 * k * tn * 4 + k * tn * 2
    for (kind, arr, layer, off) in extras:
        assert off % tn == 0
        if kind == 'tile':
            in_specs.append(pl.BlockSpec((tm, tn), lambda j, i, cb=off // tn: (i, cb + j)))
            nbytes += 2 * tm * tn * arr.dtype.itemsize
        else:
            in_specs.append(pl.BlockSpec((None, 1, tn), lambda j, i, layer=layer, cb=off // tn: (layer, 0, cb + j)))
        args.append(arr)
    out_specs = [pl.BlockSpec((tm, tn), lambda j, i: (i, j)) for _ in out_dtypes]
    out_shape = [jax.ShapeDtypeStruct((m, n_cols), dt) for dt in out_dtypes]
    nbytes += sum(2 * tm * tn * jnp.dtype(dt).itemsize for dt in out_dtypes)
    nbytes += (len(dots) + 3) * tm * tn * 4
    kernel = functools.partial(
        _fused_matmul_kernel, n_acts=len(acts), dot_act=tuple(d[0] for d in dots), forms=tuple(forms),
        n_extras=len(extras), n_outs=len(out_dtypes), epilogue=epilogue)
    return pl.pallas_call(
        kernel, grid=grid, in_specs=in_specs, out_specs=out_specs, out_shape=out_shape,
        scratch_shapes=scratch, name=name,
        compiler_params=pltpu.CompilerParams(
            dimension_semantics=("arbitrary", "arbitrary"), vmem_limit_bytes=_vmem_limit(nbytes)),
    )(*args)


def _conv_kernel(u_ref, prev_ref, w_ref, b_ref, g_ref, beta_ref, o_ref, win, shifted, ybuf, *, kw, halo, rows):
    i = pl.program_id(0)
    j = pl.program_id(1)
    tt, cc = u_ref.shape
    win[pl.ds(0, halo), :] = jnp.where(i > 0, prev_ref[...], 0.0)
    win[pl.ds(halo, tt), :] = u_ref[...]
    span = shifted.shape[1]
    for s in range(1, 8):
        shifted[s - 1] = win[pl.ds(s, span), :]
    lead = halo - (kw - 1)
    for r0 in range(0, tt, rows):
        acc = jnp.broadcast_to(b_ref[...], (rows, cc))
        for t in range(kw):
            s = (lead + t) % 8
            start = r0 + lead + t - s
            src = win[pl.ds(start, rows), :] if s == 0 else shifted[s - 1, pl.ds(start, rows), :]
            acc = acc + src * w_ref[pl.ds(t, 1), :]
        ybuf[j, pl.ds(r0, rows), :] = acc

    @pl.when(j == pl.num_programs(1) - 1)
    def _():
        nc = ybuf.shape[0]
        c = nc * cc
        s = jnp.zeros((tt, 1), F32)
        for k in range(nc):
            s = s + jnp.sum(ybuf[k], axis=-1, keepdims=True)
        mu = s * (1.0 / c)
        v = jnp.zeros((tt, 1), F32)
        for k in range(nc):
            d = ybuf[k] - mu
            v = v + jnp.sum(d * d, axis=-1, keepdims=True)
        rstd = lax.rsqrt(v * (1.0 / c) + LN_EPS)
        for k in range(nc):
            cs = slice(k * cc, (k + 1) * cc)
            y = (ybuf[k] - mu) * rstd * g_ref[:, cs] + beta_ref[:, cs]
            o_ref[:, cs] = (y * _sigmoid(y)).astype(o_ref.dtype)


def _conv_ln_silu(u, conv_w, conv_b, ln_g, ln_b, layer):
    m, c = u.shape
    kw = conv_w.shape[1]
    halo = -(-(kw - 1) // 8) * 8
    tt = _pick(m, 512, halo)
    cc = _pick(c, 256)
    rows = 64 if tt % 64 == 0 else tt
    kernel = functools.partial(_conv_kernel, kw=kw, halo=halo, rows=rows)
    return pl.pallas_call(
        kernel, grid=(m // tt, c // cc),
        in_specs=[
            pl.BlockSpec((tt, cc), lambda i, j: (i, j)),
            pl.BlockSpec((halo, cc), lambda i, j: (jnp.maximum(i * (tt // halo) - 1, 0), j)),
            pl.BlockSpec((None, kw, cc), lambda i, j: (layer, 0, j)),
            pl.BlockSpec((None, 1, cc), lambda i, j: (layer, 0, j)),
            pl.BlockSpec((None, 1, c), lambda i, j: (layer, 0, 0)),
            pl.BlockSpec((None, 1, c), lambda i, j: (layer, 0, 0)),
        ],
        out_specs=pl.BlockSpec((tt, c), lambda i, j: (i, 0)),
        out_shape=jax.ShapeDtypeStruct((m, c), BF16),
        scratch_shapes=[pltpu.VMEM((tt + halo, cc), F32), pltpu.VMEM((7, tt + halo - 8, cc), F32),
                        pltpu.VMEM((c // cc, tt, cc), F32)],
        name="conv_ln_silu",
        compiler_params=pltpu.CompilerParams(
            dimension_semantics=("arbitrary", "arbitrary"),
            vmem_limit_bytes=_vmem_limit(3 * tt * c * 4 + 14 * (tt + halo) * cc * 4)),
    )(u, u, conv_w, conv_b, ln_g, ln_b)


def _gla_kernel(q_ref, k_ref, v_ref, r_ref, a_ref, wup_ref, ba_ref, g_ref, o_ref,
                state, qe_s, qi_s, ki_s, kd_s, dec_s, obuf, *, rank, chunk):
    tb, dk = q_ref.shape
    dv = v_ref.shape[1]
    hk, hv = dk // GLA_HEADS, dv // GLA_HEADS
    nchunks = tb // chunk

    @pl.when(pl.program_id(0) == 0)
    def _():
        state[...] = jnp.zeros_like(state)

    lane = lax.broadcasted_iota(jnp.int32, a_ref.shape, 1)
    a_low = jnp.where(lane < rank, a_ref[...], 0.0)
    z = _dot_f32(a_low, wup_ref[...]) + ba_ref[...]
    la = (jnp.minimum(z, 0.0) - jnp.log(1.0 + jnp.exp(-jnp.abs(z)))) * (1.0 / GLA_GATE_TEMP)
    la_hi, la_lo, _ = _split3(la)

    rr = lax.broadcasted_iota(jnp.int32, (tb, tb), 0)
    cc = lax.broadcasted_iota(jnp.int32, (tb, tb), 1)
    sh = chunk.bit_length() - 1
    assert chunk == 1 << sh
    same = (rr >> sh) == (cc >> sh)
    tril = jnp.where(same & (cc <= rr), 1.0, 0.0).astype(BF16)
    ones_blk = jnp.where(same, 1.0, 0.0).astype(BF16)
    b = _dot(tril, la_hi) + _dot(tril, la_lo)
    bl = _dot(ones_blk, la_hi) + _dot(ones_blk, la_lo)

    sr = lax.broadcasted_iota(jnp.int32, (tb, nchunks * LANES), 0)
    sc = lax.broadcasted_iota(jnp.int32, (tb, nchunks * LANES), 1)
    sel = jnp.where((sr >> sh) == (sc >> (LANES.bit_length() - 1)), 1.0, 0.0).astype(BF16)
    tn_dims = (((0,), (0,)), ((), ()))
    tot_t = (lax.dot_general(la_hi, sel, tn_dims, preferred_element_type=F32)
             + lax.dot_general(la_lo, sel, tn_dims, preferred_element_type=F32))
    dec_s[...] = jnp.exp(tot_t)

    q = q_ref[...].astype(F32) * (float(hk) ** -0.5)
    k = k_ref[...].astype(F32)
    half = 0.5 * bl
    qe_s[...] = (q * jnp.exp(b)).astype(BF16)
    qi_s[...] = (q * jnp.exp(b - half)).astype(BF16)
    ki_s[...] = (k * jnp.exp(half - b)).astype(BF16)
    kd_s[...] = (k * jnp.exp(bl - b)).astype(BF16)

    ri = lax.broadcasted_iota(jnp.int32, (chunk, chunk), 0)
    ci = lax.broadcasted_iota(jnp.int32, (chunk, chunk), 1)
    causal = ci <= ri
    nt_dims = (((1,), (1,)), ((), ()))
    for c in range(nchunks):
        rows = pl.ds(c * chunk, chunk)
        for h in range(GLA_HEADS):
            ks = pl.ds(h * hk, hk)
            vs = pl.ds(h * hv, hv)
            vch = v_ref[rows, vs].astype(BF16)
            scores = lax.dot_general(qi_s[rows, ks], ki_s[rows, ks], nt_dims, preferred_element_type=F32)
            scores = jnp.where(causal, scores, 0.0).astype(BF16)
            s_old = state[h]
            o = _dot(qe_s[rows, ks], s_old.astype(BF16)) + _dot(scores, vch)
            obuf[rows, vs] = o
            dcol = dec_s[ks, pl.ds(c * LANES, LANES)]
            decay = jnp.concatenate([dcol] * (hv // LANES), axis=1) if hv >= LANES else dcol[:, :hv]
            state[h] = s_old * decay + lax.dot_general(kd_s[rows, ks], vch, tn_dims, preferred_element_type=F32)

    for h in range(GLA_HEADS):
        vs = pl.ds(h * hv, hv)
        o = obuf[:, vs]
        ms = jnp.mean(o * o, axis=-1, keepdims=True)
        on = o * lax.rsqrt(ms + LN_EPS) * g_ref[:, vs]
        r = r_ref[:, vs].astype(F32)
        o_ref[:, vs] = (on * (r * _sigmoid(r))).astype(o_ref.dtype)


def _gla(h2, a_low, wup_pad, b_a, norm_g, layer, *, dk, dv, rank):
    m = h2.shape[0]
    tb = _pick(m, 256, GLA_CHUNK)
    assert tb % GLA_CHUNK == 0 and dv == 2 * dk
    nchunks = tb // GLA_CHUNK
    hk, hv = dk // GLA_HEADS, dv // GLA_HEADS
    kernel = functools.partial(_gla_kernel, rank=rank, chunk=GLA_CHUNK)
    return pl.pallas_call(
        kernel, grid=(m // tb,),
        in_specs=[
            pl.BlockSpec((tb, dk), lambda i: (i, 0)),
            pl.BlockSpec((tb, dk), lambda i: (i, 1)),
            pl.BlockSpec((tb, dv), lambda i: (i, 1)),
            pl.BlockSpec((tb, dv), lambda i: (i, 2)),
            pl.BlockSpec((tb, LANES), lambda i: (i, 0)),
            pl.BlockSpec((None, LANES, dk), lambda i: (layer, 0, 0)),
            pl.BlockSpec((None, 1, dk), lambda i: (layer, 0, 0)),
            pl.BlockSpec((None, 1, dv), lambda i: (layer, 0, 0)),
        ],
        out_specs=pl.BlockSpec((tb, dv), lambda i: (i, 0)),
        out_shape=jax.ShapeDtypeStruct((m, dv), BF16),
        scratch_shapes=[
            pltpu.VMEM((GLA_HEADS, hk, hv), F32),
            pltpu.VMEM((tb, dk), BF16), pltpu.VMEM((tb, dk), BF16),
            pltpu.VMEM((tb, dk), BF16), pltpu.VMEM((tb, dk), BF16),
            pltpu.VMEM((dk, nchunks * LANES), F32),
            pltpu.VMEM((tb, dv), F32),
        ],
        name="gla",
        compiler_params=pltpu.CompilerParams(
            dimension_semantics=("arbitrary",), vmem_limit_bytes=_vmem_limit(48 << 20)),
    )(h2, h2, h2, h2, a_low, wup_pad, b_a, norm_g)


def _pack_halves(x):
    n = x.shape[1] // 2
    hi = lax.bitcast_convert_type(x[:, :n].astype(BF16).astype(F32), jnp.uint32)
    lo = lax.bitcast_convert_type(x[:, n:].astype(BF16).astype(F32), jnp.uint32)
    return hi | (lo >> 16)


def _unpack_halves(w):
    hi = lax.bitcast_convert_type(w & jnp.uint32(0xFFFF0000), F32)
    lo = lax.bitcast_convert_type(w << 16, F32)
    return hi, lo


def _ln_kernel(z_ref, g_ref, b_ref, *rest, with_router):
    if with_router:
        wr_ref, br_ref, x_ref, xp_ref, lg_ref = rest
    else:
        x_ref, *xb_refs = rest
    z = z_ref[...]
    mu = jnp.mean(z, axis=-1, keepdims=True)
    d = z - mu
    var = jnp.mean(d * d, axis=-1, keepdims=True)
    x = d * lax.rsqrt(var + LN_EPS) * g_ref[...] + b_ref[...]
    x_ref[...] = x
    if with_router:
        xp_ref[...] = _pack_halves(x)
        lg_ref[...] = _dot_f32(x, wr_ref[...]) + br_ref[...]
    for xb_ref in ([] if with_router else xb_refs):
        xb_ref[...] = x.astype(BF16)


def _layer_norm(z, ln_g, ln_b, idx, router=None, bf16_copy=True):
    m, d = z.shape
    tr = _pick(m, 512, 8)
    in_specs = [
        pl.BlockSpec((tr, d), lambda i: (i, 0)),
        pl.BlockSpec((None, 1, d), lambda i: (idx, 0, 0)),
        pl.BlockSpec((None, 1, d), lambda i: (idx, 0, 0)),
    ]
    args = [z, ln_g, ln_b]
    if router is None:
        out_specs = [pl.BlockSpec((tr, d), lambda i: (i, 0))] * (2 if bf16_copy else 1)
        out_shape = [jax.ShapeDtypeStruct((m, d), F32), jax.ShapeDtypeStruct((m, d), BF16)][:len(out_specs)]
    else:
        in_specs += [pl.BlockSpec((d, LANES), lambda i: (0, 0)), pl.BlockSpec((1, LANES), lambda i: (0, 0))]
        args += list(router)
        out_specs = [pl.BlockSpec((tr, d), lambda i: (i, 0)), pl.BlockSpec((tr, d // 2), lambda i: (i, 0)),
                     pl.BlockSpec((tr, LANES), lambda i: (i, 0))]
        out_shape = [jax.ShapeDtypeStruct((m, d), F32), jax.ShapeDtypeStruct((m, d // 2), jnp.uint32),
                     jax.ShapeDtypeStruct((m, LANES), F32)]
    return pl.pallas_call(
        functools.partial(_ln_kernel, with_router=router is not None),
        grid=(m // tr,), in_specs=in_specs, out_specs=out_specs, out_shape=out_shape,
        name="layer_norm_router" if router is not None else "layer_norm",
        compiler_params=pltpu.CompilerParams(
            dimension_semantics=("arbitrary",), vmem_limit_bytes=_vmem_limit(tr * d * 36)),
    )(*args)


def _route_kernel(lg_ref, idx_ref, w_ref, rank_ref, cnt_ref, *, n_experts):
    per = n_experts // N_GROUPS
    logit = [lg_ref[e] for e in range(n_experts)]
    mx = functools.reduce(jnp.maximum, logit)
    ex = [jnp.exp(l - mx) for l in logit]
    tot = functools.reduce(lambda a, c: a + c, ex)
    p = [e / tot for e in ex]

    def top2_sum(vals):
        a, bb, c, d = vals
        hi1, lo1 = jnp.maximum(a, bb), jnp.minimum(a, bb)
        hi2, lo2 = jnp.maximum(c, d), jnp.minimum(c, d)
        first = jnp.maximum(hi1, hi2)
        second = jnp.maximum(jnp.minimum(hi1, hi2), jnp.maximum(lo1, lo2))
        return first + second

    assert per == 4
    gs = [top2_sum(p[g * per:(g + 1) * per]) for g in range(N_GROUPS)]
    best = jnp.zeros_like(gs[0], dtype=jnp.int32)
    best_v = gs[0]
    for g in range(1, N_GROUPS):
        upd = gs[g] > best_v
        best = jnp.where(upd, g, best)
        best_v = jnp.where(upd, gs[g], best_v)
    vals = []
    for k in range(per):
        v = p[k]
        for g in range(1, N_GROUPS):
            v = jnp.where(best == g, p[g * per + k], v)
        vals.append(v)
    i1 = jnp.zeros_like(best)
    v1 = vals[0]
    for k in range(1, per):
        upd = vals[k] > v1
        i1 = jnp.where(upd, k, i1)
        v1 = jnp.where(upd, vals[k], v1)
    i2 = jnp.full_like(best, -1)
    v2 = jnp.full_like(v1, -1.0)
    for k in range(per):
        upd = (i1 != k) & (vals[k] > v2)
        i2 = jnp.where(upd, k, i2)
        v2 = jnp.where(upd, vals[k], v2)
    den = v1 + v2
    picks = (best * per + i1, best * per + i2)
    idx_ref[0] = picks[0]
    idx_ref[1] = picks[1]
    w_ref[0] = v1 / den
    w_ref[1] = v2 / den

    rows, lanes = best.shape
    li = lax.broadcasted_iota(jnp.int32, (lanes, lanes), 0)
    lj = lax.broadcasted_iota(jnp.int32, (lanes, lanes), 1)
    upper = jnp.where(li <= lj, 1.0, 0.0).astype(BF16)
    ones = jnp.ones((lanes, lanes), BF16)
    ri = lax.broadcasted_iota(jnp.int32, (rows, rows), 0)
    rj = lax.broadcasted_iota(jnp.int32, (rows, rows), 1)
    before = jnp.where(rj < ri, 1.0, 0.0).astype(BF16)
    ranks = [jnp.zeros((rows, lanes), F32), jnp.zeros((rows, lanes), F32)]
    for e in range(n_experts):
        base = jnp.zeros((1, lanes), F32)
        for s in range(2):
            hit = picks[s] == e
            ind = jnp.where(hit, 1.0, 0.0).astype(BF16)
            in_row = _dot(ind, upper)
            row_tot = _dot(ind, ones)
            row_off = _dot(before, row_tot.astype(BF16))
            ranks[s] = jnp.where(hit, in_row + row_off + (base - 1.0), ranks[s])
            base = base + (row_off + row_tot)[rows - 1:rows, :]
        cnt_ref[pl.ds(e, 1), :] = base.astype(jnp.int32)
    rank_ref[0] = ranks[0].astype(jnp.int32)
    rank_ref[1] = ranks[1].astype(jnp.int32)


def _route(logits_t):
    n_experts, rows, lanes = logits_t.shape
    return pl.pallas_call(
        functools.partial(_route_kernel, n_experts=n_experts),
        out_shape=[jax.ShapeDtypeStruct((2, rows, lanes), jnp.int32),
                   jax.ShapeDtypeStruct((2, rows, lanes), F32),
                   jax.ShapeDtypeStruct((2, rows, lanes), jnp.int32),
                   jax.ShapeDtypeStruct((n_experts, lanes), jnp.int32)],
        name="route",
    )(logits_t)


def _dispatch_kernel(pos_ref, tr_ref, x_ref, o_hbm, zeros, sem, zsem, *, m, tm):
    i = pl.program_id(0)
    chunk = x_ref.shape[0]
    part = zeros.shape[0]

    @pl.when(i == 0)
    def _():
        zeros[...] = jnp.zeros_like(zeros)
        for wait in (False, True):
            for tile in range(tr_ref.shape[0]):
                for q in range(tm // part):
                    @pl.when(tr_ref[tile] < (q + 1) * part)
                    def _(tile=tile, q=q, wait=wait):
                        fill = pltpu.make_async_copy(
                            zeros, o_hbm.at[pl.ds(tile * tm + q * part, part), :], zsem)
                        fill.wait() if wait else fill.start()

    def row_copy(r, dst):
        return pltpu.make_async_copy(x_ref.at[pl.ds(r, 1), :], o_hbm.at[pl.ds(dst, 1), :], sem)

    def issue(r, carry):
        row_copy(r, pos_ref[i * chunk + r]).start()
        row_copy(r, pos_ref[m + i * chunk + r]).start()
        return carry
    lax.fori_loop(0, chunk, issue, 0, unroll=8)

    def drain(r, carry):
        row_copy(r, 0).wait()
        row_copy(r, 0).wait()
        return carry
    lax.fori_loop(0, chunk, drain, 0, unroll=8)


def _dispatch_rows(xp, pos, tile_rows, *, tm):
    m, n = xp.shape
    chunk = _pick(m, 512, 8)
    return pl.pallas_call(
        functools.partial(_dispatch_kernel, m=m, tm=tm),
        grid_spec=pltpu.PrefetchScalarGridSpec(
            num_scalar_prefetch=2, grid=(m // chunk,),
            in_specs=[pl.BlockSpec((chunk, n), lambda i, pos, tr: (i, 0))],
            out_specs=pl.BlockSpec(memory_space=pl.ANY),
            scratch_shapes=[pltpu.VMEM((tm // ROW_PARTS, n), xp.dtype),
                            pltpu.SemaphoreType.DMA(()), pltpu.SemaphoreType.DMA(())]),
        out_shape=jax.ShapeDtypeStruct((tile_rows.shape[0] * tm, n), xp.dtype),
        name="expert_dispatch",
        compiler_params=pltpu.CompilerParams(dimension_semantics=("arbitrary",)),
    )(pos, tile_rows, xp)


def _stream_expert_weights(te_ref, nx_ref, live, w_hbm, stage, w_bf16, sem, layer):
    j, t = pl.program_id(0), pl.program_id(1)
    n_col_tiles = pl.num_programs(0)

    def fetch(k, e, jj):
        tw = stage[k].shape[1]
        src = w_hbm[k].at[layer, e, :, pl.ds(pl.multiple_of(jj * tw, tw), tw)]
        return pltpu.make_async_copy(src, stage[k], sem.at[k])

    @pl.when((j == 0) & (t == 0))
    def _():
        for k in range(len(stage)):
            fetch(k, te_ref[0], 0).start()

    @pl.when(live & ((t == 0) | (te_ref[t] != te_ref[jnp.maximum(t - 1, 0)])))
    def _():
        for k in range(len(stage)):
            fetch(k, te_ref[t], j).wait()
        rows = stage[0].shape[0]
        step = _pick(rows, 512, 16)

        def convert(c, carry):
            rs = pl.ds(pl.multiple_of(c * step, step), step)
            for k in range(len(stage)):
                w_bf16[k][rs, :] = stage[k][rs, :].astype(BF16)
            return carry
        lax.fori_loop(0, rows // step, convert, 0)
        nxt = nx_ref[t]

        @pl.when(nxt >= 0)
        def _():
            for k in range(len(stage)):
                fetch(k, nxt, j).start()

        @pl.when((nxt < 0) & (j + 1 < n_col_tiles))
        def _():
            for k in range(len(stage)):
                fetch(k, te_ref[0], j + 1).start()


ROW_PARTS = 4


def _compute_valid_rows(live, n_valid, compute, o_ref):
    tm = o_ref.shape[0]
    part = tm // ROW_PARTS
    for nr in range(part, tm + 1, part):
        @pl.when(live & (n_valid > nr - part) & (n_valid <= nr))
        def _(nr=nr):
            compute(nr)
            if nr < tm:
                o_ref[pl.ds(nr, tm - nr), :] = jnp.zeros((tm - nr, o_ref.shape[1]), o_ref.dtype)

    @pl.when(jnp.logical_not(live))
    def _():
        o_ref[...] = jnp.zeros_like(o_ref)


def _expert_up_kernel(te_ref, tr_ref, nt_ref, nx_ref, x_ref, wg_hbm, wu_hbm, o_ref,
                      wg_st, wu_st, wg_s, wu_s, sem, *, layer):
    t = pl.program_id(1)
    live = t < nt_ref[0]
    n_valid = tr_ref[t]
    half_d = x_ref.shape[1]
    _stream_expert_weights(te_ref, nx_ref, live, (wg_hbm, wu_hbm), (wg_st, wu_st), (wg_s, wu_s), sem, layer)

    def compute(nr):
        x_hi, x_lo = _unpack_halves(x_ref[pl.ds(0, nr), :])
        x_hi, x_lo = x_hi.astype(BF16), x_lo.astype(BF16)
        top, bot = pl.ds(0, half_d), pl.ds(half_d, half_d)
        g = _dot(x_hi, wg_s[top, :]) + _dot(x_lo, wg_s[bot, :])
        u = _dot(x_hi, wu_s[top, :]) + _dot(x_lo, wu_s[bot, :])
        o_ref[pl.ds(0, nr), :] = (g * _sigmoid(g) * u).astype(o_ref.dtype)

    _compute_valid_rows(live, n_valid, compute, o_ref)


def _live_tile(j, t, te, tr, nt, nx):
    return (jnp.minimum(t, nt[0] - 1), 0)


def _expert_up(xg, w_gate, w_up, layer, sched, *, tm):
    npad, half_d = xg.shape
    d = 2 * half_d
    f = w_gate.shape[-1]
    tf = _pick(f, 512)
    return pl.pallas_call(
        functools.partial(_expert_up_kernel, layer=layer),
        grid_spec=pltpu.PrefetchScalarGridSpec(
            num_scalar_prefetch=4, grid=(f // tf, npad // tm),
            in_specs=[pl.BlockSpec((tm, half_d), _live_tile),
                      pl.BlockSpec(memory_space=pl.ANY), pl.BlockSpec(memory_space=pl.ANY)],
            out_specs=pl.BlockSpec((tm, tf), lambda j, t, te, tr, nt, nx: (t, j)),
            scratch_shapes=[pltpu.VMEM((d, tf), F32), pltpu.VMEM((d, tf), F32),
                            pltpu.VMEM((d, tf), BF16), pltpu.VMEM((d, tf), BF16),
                            pltpu.SemaphoreType.DMA((2,))]),
        out_shape=jax.ShapeDtypeStruct((npad, f), BF16),
        name="expert_up",
        compiler_params=pltpu.CompilerParams(
            dimension_semantics=("arbitrary", "arbitrary"),
            vmem_limit_bytes=_vmem_limit(2 * d * tf * 4 + 2 * d * tf * 2 + 5 * tm * d * 2 + 8 * tm * tf * 4)),
    )(*sched, xg, w_gate, w_up)


def _expert_down_kernel(te_ref, tr_ref, nt_ref, nx_ref, h_ref, wd_hbm, o_ref, wd_st, wd_s, sem, *, layer):
    t = pl.program_id(1)
    live = t < nt_ref[0]
    n_valid = tr_ref[t]
    _stream_expert_weights(te_ref, nx_ref, live, (wd_hbm,), (wd_st,), (wd_s,), sem, layer)

    def compute(nr):
        o_ref[pl.ds(0, nr), :] = _pack_halves(_dot(h_ref[pl.ds(0, nr), :], wd_s[...]))

    _compute_valid_rows(live, n_valid, compute, o_ref)


def _expert_down(hg, w_down, layer, sched, *, tm, tn):
    npad, f = hg.shape
    d = w_down.shape[-1]
    return pl.pallas_call(
        functools.partial(_expert_down_kernel, layer=layer),
        grid_spec=pltpu.PrefetchScalarGridSpec(
            num_scalar_prefetch=4, grid=(d // tn, npad // tm),
            in_specs=[pl.BlockSpec((tm, f), _live_tile), pl.BlockSpec(memory_space=pl.ANY)],
            out_specs=pl.BlockSpec((tm, tn // 2), lambda j, t, te, tr, nt, nx: (t, j)),
            scratch_shapes=[pltpu.VMEM((f, tn), F32), pltpu.VMEM((f, tn), BF16),
                            pltpu.SemaphoreType.DMA((1,))]),
        out_shape=jax.ShapeDtypeStruct((npad, d // 2), jnp.uint32),
        name="expert_down",
        compiler_params=pltpu.CompilerParams(
            dimension_semantics=("arbitrary", "arbitrary"),
            vmem_limit_bytes=_vmem_limit(f * tn * 4 + f * tn * 2 + 2 * tm * f * 2 + 6 * tm * tn * 4)),
    )(*sched, hg, w_down)


def _combine_ln_kernel(pos_ref, y_hbm, x_ref, w0_ref, w1_ref, g_ref, b_ref, o_ref, ob_ref, buf, sem,
                       *, alpha, m, tn):
    i = pl.program_id(0)
    tc, d = x_ref.shape

    def row_copy(slot, s, r, p):
        return pltpu.make_async_copy(y_hbm.at[pl.ds(p, 1), :], buf.at[slot, s, pl.ds(r, 1), :], sem.at[slot])

    def gather(step):
        def issue(r, carry):
            row_copy(step % 2, 0, r, pos_ref[step * tc + r]).start()
            row_copy(step % 2, 1, r, pos_ref[m + step * tc + r]).start()
            return carry
        lax.fori_loop(0, tc, issue, 0, unroll=8)

    @pl.when(i == 0)
    def _():
        gather(i)

    @pl.when(i + 1 < pl.num_programs(0))
    def _():
        gather(i + 1)

    def drain(r, carry):
        row_copy(i % 2, 0, r, 0).wait()
        row_copy(i % 2, 1, r, 0).wait()
        return carry
    lax.fori_loop(0, tc, drain, 0, unroll=8)

    w0, w1 = w0_ref[...], w1_ref[...]
    rows = buf.at[i % 2]
    parts = []
    for j in range(d // tn):
        cs = pl.ds(j * (tn // 2), tn // 2)
        hi0, lo0 = _unpack_halves(rows[0, :, cs])
        hi1, lo1 = _unpack_halves(rows[1, :, cs])
        parts += [w0 * hi0 + w1 * hi1, w0 * lo0 + w1 * lo1]
    z = alpha * x_ref[...] + jnp.concatenate(parts, axis=1)
    mu = jnp.mean(z, axis=-1, keepdims=True)
    dz = z - mu
    var = jnp.mean(dz * dz, axis=-1, keepdims=True)
    x = dz * lax.rsqrt(var + LN_EPS) * g_ref[...] + b_ref[...]
    o_ref[...] = x
    ob_ref[...] = x.astype(BF16)


def _combine_ln(yg, pos, ewts, x, ln_g, ln_b, idx, *, alpha, tn):
    m, d = x.shape
    tc = _pick(m, 256, 8)
    row = lambda i, pos: (i, 0)
    par = lambda i, pos: (idx, 0, 0)
    return pl.pallas_call(
        functools.partial(_combine_ln_kernel, alpha=alpha, m=m, tn=tn),
        grid_spec=pltpu.PrefetchScalarGridSpec(
            num_scalar_prefetch=1, grid=(m // tc,),
            in_specs=[pl.BlockSpec(memory_space=pl.ANY),
                      pl.BlockSpec((tc, d), row),
                      pl.BlockSpec((None, tc, 1), lambda i, pos: (0, i, 0)),
                      pl.BlockSpec((None, tc, 1), lambda i, pos: (1, i, 0)),
                      pl.BlockSpec((None, 1, d), par), pl.BlockSpec((None, 1, d), par)],
            out_specs=[pl.BlockSpec((tc, d), row), pl.BlockSpec((tc, d), row)],
            scratch_shapes=[pltpu.VMEM((2, 2, tc, d // 2), jnp.uint32), pltpu.SemaphoreType.DMA((2,))]),
        out_shape=[jax.ShapeDtypeStruct((m, d), F32), jax.ShapeDtypeStruct((m, d), BF16)],
        name="combine_ln",
        compiler_params=pltpu.CompilerParams(
            dimension_semantics=("arbitrary",), vmem_limit_bytes=_vmem_limit(tc * d * 48)),
    )(pos, yg, x, ewts, ewts, ln_g, ln_b)


def _expert_schedule(eidx, rank, counts, tm, n_tiles_max):
    n_experts = counts.shape[0]
    tiles = (counts + tm - 1) // tm
    tile_end = jnp.cumsum(tiles)
    tile_start = tile_end - tiles
    experts = jnp.arange(n_experts, dtype=jnp.int32)
    pick = eidx[:, :, None] == experts
    pos = rank + jnp.sum(jnp.where(pick, tile_start * tm, 0), axis=-1)
    n_tiles = tile_end[-1]
    tile_ids = jnp.arange(n_tiles_max, dtype=jnp.int32)
    tile_e = jnp.sum(tile_end[None, :] <= jnp.minimum(tile_ids, n_tiles - 1)[:, None], axis=-1)
    mine = tile_e[:, None] == experts
    filled = jnp.sum(jnp.where(mine, counts - (tile_ids[:, None] - tile_start) * tm, 0), axis=-1)
    tile_rows = jnp.clip(filled, 0, tm)
    after = jnp.sum(jnp.where(mine, tile_end, 0), axis=-1)
    expert_after = jnp.sum(tile_end[None, :] <= after[:, None], axis=-1)
    next_e = jnp.where(after < n_tiles, expert_after, -1)
    i32 = lambda a: a.astype(jnp.int32)
    return i32(pos).reshape(-1), (i32(tile_e), i32(tile_rows), i32(n_tiles).reshape(1), i32(next_e))


def kernel(x, p, w_in, b_branch_gate, w_gla_gate_up, b_gla_gate, conv_w, conv_b, conv_ln_g, conv_ln_b, w_conv_out, gla_norm_g, w_gla_out, w_out, ln_g, ln_b, w_router, b_router, w_exp_gate, w_exp_up, w_exp_down, w_ple, w_ple_gate):
    bsz, seq, d = x.shape
    assert bsz == 1
    depth = w_in.shape[0]
    m = bsz * seq
    conv_ch = conv_w.shape[-1]
    rank, dk = w_gla_gate_up.shape[1], w_gla_gate_up.shape[2]
    dv = gla_norm_g.shape[-1]
    n_experts = w_router.shape[1]
    off_b = conv_ch
    off_q = 2 * conv_ch
    off_alpha = off_q + 2 * dk + 2 * dv
    off_gate = off_alpha + rank
    alpha = (2.0 * depth) ** 0.25
    tm_e = 512 if m >= 4096 else 64
    n_tiles_max = (2 * m) // tm_e + n_experts
    tn_down = _pick(d, 2048)

    xf = x.reshape(m, d)
    xb = xf.astype(BF16)
    pb = p.reshape(depth, m, p.shape[-1]).astype(BF16)
    w_in_t = jnp.swapaxes(w_in, 1, 2)
    b_gate = b_branch_gate.reshape(depth, 1, 2 * d)
    wup_pad = jnp.zeros((depth, LANES, dk), F32).at[:, :rank, :].set(w_gla_gate_up)
    b_a = b_gla_gate.reshape(depth, 1, dk)
    conv_b3 = conv_b.reshape(depth, 1, conv_ch)
    cln_g = conv_ln_g.reshape(depth, 1, conv_ch)
    cln_b = conv_ln_b.reshape(depth, 1, conv_ch)
    gnorm = gla_norm_g.reshape(depth, 1, dv)
    ln_g3 = ln_g.reshape(depth * 3, 1, d)
    ln_b3 = ln_b.reshape(depth * 3, 1, d)
    wr_pad = jnp.zeros((d, LANES), F32).at[:, :n_experts].set(w_router)
    br_pad = jnp.zeros((1, LANES), F32).at[0, :n_experts].set(b_router)

    tm = _pick(m, 1024, 8)
    for i in range(depth):
        (u,) = _fused_matmul(
            [xb], [(0, w_in_t, i, 0, True), (0, w_in_t, i, off_b, True)], [],
            lambda ds, ex: [ds[0] * _sigmoid(ds[1])], [F32],
            n_cols=conv_ch, tm=tm, tn=_pick(conv_ch, 256), name="in_proj_glu")
        (h2,) = _fused_matmul(
            [xb], [(0, w_in_t, i, off_q, True)], [], lambda ds, ex: [ds[0]], [F32],
            n_cols=2 * dk + 2 * dv, tm=tm, tn=_pick(dk, 512), name="in_proj_qkvr")
        (a_low,) = _fused_matmul(
            [xb], [(0, w_in_t, i, off_alpha, True)], [], lambda ds, ex: [ds[0]], [F32],
            n_cols=LANES, tm=tm, tn=LANES, name="in_proj_alpha")
        (gates,) = _fused_matmul(
            [xb], [(0, w_in_t, i, off_gate, True)], [('row', b_gate, i, 0)],
            lambda ds, ex: [_sigmoid(ds[0] + ex[0])], [BF16],
            n_cols=2 * d, tm=tm, tn=_pick(d, 512), name="in_proj_gates")

        uc = _conv_ln_silu(u, conv_w, conv_b3, cln_g, cln_b, i)
        og = _gla(h2, a_low, wup_pad, b_a, gnorm, i, dk=dk, dv=dv, rank=rank)

        tn_o = _pick(d, 512)
        (merged,) = _fused_matmul(
            [uc, og], [(0, w_conv_out, i, 0, False), (1, w_gla_out, i, 0, False)],
            [('tile', gates, None, 0), ('tile', gates, None, d)],
            lambda ds, ex: [ex[0].astype(F32) * ds[0] + ex[1].astype(F32) * ds[1]], [BF16],
            n_cols=d, tm=tm, tn=tn_o, name="branch_out_merge")
        (z1,) = _fused_matmul(
            [merged], [(0, w_out, i, 0, False)], [('tile', xf, None, 0)],
            lambda ds, ex: [alpha * ex[0] + ds[0]], [F32],
            n_cols=d, tm=tm, tn=tn_o, name="mixer_out")
        x1, x1p, logits = _layer_norm(z1, ln_g3, ln_b3, 3 * i, router=(wr_pad, br_pad))

        logits_t = logits[:, :n_experts].T.reshape(n_experts, m // LANES, LANES)
        eidx, ewts, pair_rank, counts = _route(logits_t)
        pos, sched = _expert_schedule(
            eidx.reshape(2, m), pair_rank.reshape(2, m), counts[:, 0], tm_e, n_tiles_max)
        xg = _dispatch_rows(x1p, pos, sched[1], tm=tm_e)
        hg = _expert_up(xg, w_exp_gate, w_exp_up, i, sched, tm=tm_e)
        yg = _expert_down(hg, w_exp_down, i, sched, tm=tm_e, tn=tn_down)
        x2, x2b = _combine_ln(yg, pos, ewts.reshape(2, m, 1), x1, ln_g3, ln_b3, 3 * i + 1,
                              alpha=alpha, tn=tn_down)

        (z3,) = _fused_matmul(
            [x2b, pb[i]], [(0, w_ple_gate, i, 0, False), (1, w_ple, i, 0, False)], [('tile', x2, None, 0)],
            lambda ds, ex: [alpha * ex[0] + _sigmoid(ds[0]) * ds[1]], [F32],
            n_cols=d, tm=tm, tn=tn_o, name="ple")
        xf, *xb = _layer_norm(z3, ln_g3, ln_b3, 3 * i + 2, bf16_copy=i + 1 < depth)
        xb = xb[0] if xb else None

    return xf.reshape(bsz, seq, d)
```

```python
import functools

import jax
import jax.numpy as jnp
from jax import lax
from jax.experimental import pallas as pl
from jax.experimental.pallas import tpu as pltpu

F32 = jnp.float32
BF16 = jnp.bfloat16

GLA_HEADS = 4
GLA_GATE_TEMP = 16.0
GLA_CHUNK = 64
N_GROUPS = 4
LN_EPS = 1e-5
LANES = 128
V7X_SCOPED_VMEM_BYTES = 60000 * 1024


def _pick(n, pref, mult=LANES):
    if n <= pref:
        return n
    t = (pref // mult) * mult
    while t >= mult:
        if n % t == 0:
            return t
        t -= mult
    return n


def _vmem_limit(nbytes):
    return int(min(max(nbytes * 5 // 4 + (4 << 20), 16 << 20), V7X_SCOPED_VMEM_BYTES))


def _split3(x):
    hi = x.astype(BF16)
    r1 = x - hi.astype(F32)
    mid = r1.astype(BF16)
    lo = (r1 - mid.astype(F32)).astype(BF16)
    return hi, mid, lo


def _dot(a, b):
    return jnp.dot(a, b, preferred_element_type=F32)


def _sigmoid(x):
    return 0.5 * jnp.tanh(0.5 * x) + 0.5


def _dot_f32(a, b):
    ah, am, _ = _split3(a)
    bh, bm, _ = _split3(b)
    return _dot(ah, bh) + (_dot(ah, bm) + _dot(am, bh))


NT_DIMS = (((1,), (1,)), ((), ()))


def _fused_matmul_kernel(*refs, n_acts, dot_act, forms, n_extras, n_outs, epilogue):
    n_w = sum(2 if shift else 1 for _, shift in forms)
    act_refs = refs[:n_acts]
    w_refs = list(refs[n_acts:n_acts + n_w])
    ex_refs = refs[n_acts + n_w:n_acts + n_w + n_extras]
    out_refs = refs[n_acts + n_w + n_extras:n_acts + n_w + n_extras + n_outs]
    w_bf16 = refs[n_acts + n_w + n_extras + n_outs:]

    @pl.when(pl.program_id(1) == 0)
    def _():
        for s, (_, shift) in zip(w_bf16, forms):
            w_ref = w_refs.pop(0)
            if shift:
                tn = s.shape[0]
                s[pl.ds(0, tn - shift), :] = w_ref[pl.ds(shift, tn - shift), :].astype(BF16)
                s[pl.ds(tn - shift, shift), :] = w_refs.pop(0)[...].astype(BF16)
            else:
                s[...] = w_ref[...].astype(BF16)

    dots = []
    for ai, s, (output_major, _) in zip(dot_act, w_bf16, forms):
        a = act_refs[ai][...]
        dots.append(lax.dot_general(a, s[...], NT_DIMS, preferred_element_type=F32) if output_major
                    else _dot(a, s[...]))
    outs = epilogue(dots, [e[...] for e in ex_refs])
    for o_ref, o in zip(out_refs, outs):
        o_ref[...] = o.astype(o_ref.dtype)


def _fused_matmul(acts, dots, extras, epilogue, out_dtypes, *, n_cols, tm, tn, name):
    m = acts[0].shape[0]
    assert m % tm == 0 and n_cols % tn == 0
    grid = (n_cols // tn, m // tm)
    in_specs, args, scratch, forms = [], [], [], []
    nbytes = 0
    for a in acts:
        k = a.shape[1]
        in_specs.append(pl.BlockSpec((tm, k), lambda j, i: (i, 0)))
        args.append(a)
        nbytes += 2 * tm * k * a.dtype.itemsize
    for (_, w, layer, off, output_major) in dots:
        shift = off % tn
        if output_major:
            k = w.shape[2]
            in_specs.append(pl.BlockSpec((None, tn, k), lambda j, i, layer=layer, cb=off // tn: (layer, cb + j, 0)))
            args.append(w)
            if shift:
                assert shift % 16 == 0 and tn % shift == 0 and (off - shift) % shift == 0
                in_specs.append(pl.BlockSpec(
                    (None, shift, k),
                    lambda j, i, layer=layer, cb=off // tn, r=tn // shift: (layer, (cb + j + 1) * r, 0)))
                args.append(w)
                nbytes += 2 * shift * k * 4
            scratch.append(pltpu.VMEM((tn, k), BF16))
        else:
            assert shift == 0
            k = w.shape[1]
            in_specs.append(pl.BlockSpec((None, k, tn), lambda j, i, layer=layer, cb=off // tn: (layer, 0, cb + j)))
            args.append(w)
            scratch.append(pltpu.VMEM((k, tn), BF16))
        forms.append((output_major, shift))
        nbytes += 2 * k * tn * 4 + k * tn * 2
    for (kind, arr, layer, off) in extras:
        assert off % tn == 0
        if kind == 'tile':
            in_specs.append(pl.BlockSpec((tm, tn), lambda j, i, cb=off // tn: (i, cb + j)))
            nbytes += 2 * tm * tn * arr.dtype.itemsize
        else:
            in_specs.append(pl.BlockSpec((None, 1, tn), lambda j, i, layer=layer, cb=off // tn: (layer, 0, cb + j)))
        args.append(arr)
    out_specs = [pl.BlockSpec((tm, tn), lambda j, i: (i, j)) for _ in out_dtypes]
    out_shape = [jax.ShapeDtypeStruct((m, n_cols), dt) for dt in out_dtypes]
    nbytes += sum(2 * tm * tn * jnp.dtype(dt).itemsize for dt in out_dtypes)
    nbytes += (len(dots) + 3) * tm * tn * 4
    kernel = functools.partial(
        _fused_matmul_kernel, n_acts=len(acts), dot_act=tuple(d[0] for d in dots), forms=tuple(forms),
        n_extras=len(extras), n_outs=len(out_dtypes), epilogue=epilogue)
    return pl.pallas_call(
        kernel, grid=grid, in_specs=in_specs, out_specs=out_specs, out_shape=out_shape,
        scratch_shapes=scratch, name=name,
        compiler_params=pltpu.CompilerParams(
            dimension_semantics=("arbitrary", "arbitrary"), vmem_limit_bytes=_vmem_limit(nbytes)),
    )(*args)


def _conv_kernel(u_ref, prev_ref, w_ref, b_ref, g_ref, beta_ref, o_ref, win, shifted, ybuf, *, kw, halo, rows):
    i = pl.program_id(0)
    j = pl.program_id(1)
    tt, cc = u_ref.shape
    win[pl.ds(0, halo), :] = jnp.where(i > 0, prev_ref[...], 0.0)
    win[pl.ds(halo, tt), :] = u_ref[...]
    span = shifted.shape[1]
    for s in range(1, 8):
        shifted[s - 1] = win[pl.ds(s, span), :]
    lead = halo - (kw - 1)
    for r0 in range(0, tt, rows):
        acc = jnp.broadcast_to(b_ref[...], (rows, cc))
        for t in range(kw):
            s = (lead + t) % 8
            start = r0 + lead + t - s
            src = win[pl.ds(start, rows), :] if s == 0 else shifted[s - 1, pl.ds(start, rows), :]
            acc = acc + src * w_ref[pl.ds(t, 1), :]
        ybuf[j, pl.ds(r0, rows), :] = acc

    @pl.when(j == pl.num_programs(1) - 1)
    def _():
        nc = ybuf.shape[0]
        c = nc * cc
        s = jnp.zeros((tt, 1), F32)
        for k in range(nc):
            s = s + jnp.sum(ybuf[k], axis=-1, keepdims=True)
        mu = s * (1.0 / c)
        v = jnp.zeros((tt, 1), F32)
        for k in range(nc):
            d = ybuf[k] - mu
            v = v + jnp.sum(d * d, axis=-1, keepdims=True)
        rstd = lax.rsqrt(v * (1.0 / c) + LN_EPS)
        for k in range(nc):
            cs = slice(k * cc, (k + 1) * cc)
            y = (ybuf[k] - mu) * rstd * g_ref[:, cs] + beta_ref[:, cs]
            o_ref[:, cs] = (y * _sigmoid(y)).astype(o_ref.dtype)


def _conv_ln_silu(u, conv_w, conv_b, ln_g, ln_b, layer):
    m, c = u.shape
    kw = conv_w.shape[1]
    halo = -(-(kw - 1) // 8) * 8
    tt = _pick(m, 512, halo)
    cc = _pick(c, 256)
    rows = 64 if tt % 64 == 0 else tt
    kernel = functools.partial(_conv_kernel, kw=kw, halo=halo, rows=rows)
    return pl.pallas_call(
        kernel, grid=(m // tt, c // cc),
        in_specs=[
            pl.BlockSpec((tt, cc), lambda i, j: (i, j)),
            pl.BlockSpec((halo, cc), lambda i, j: (jnp.maximum(i * (tt // halo) - 1, 0), j)),
            pl.BlockSpec((None, kw, cc), lambda i, j: (layer, 0, j)),
            pl.BlockSpec((None, 1, cc), lambda i, j: (layer, 0, j)),
            pl.BlockSpec((None, 1, c), lambda i, j: (layer, 0, 0)),
            pl.BlockSpec((None, 1, c), lambda i, j: (layer, 0, 0)),
        ],
        out_specs=pl.BlockSpec((tt, c), lambda i, j: (i, 0)),
        out_shape=jax.ShapeDtypeStruct((m, c), BF16),
        scratch_shapes=[pltpu.VMEM((tt + halo, cc), F32), pltpu.VMEM((7, tt + halo - 8, cc), F32),
                        pltpu.VMEM((c // cc, tt, cc), F32)],
        name="conv_ln_silu",
        compiler_params=pltpu.CompilerParams(
            dimension_semantics=("arbitrary", "arbitrary"),
            vmem_limit_bytes=_vmem_limit(3 * tt * c * 4 + 14 * (tt + halo) * cc * 4)),
    )(u, u, conv_w, conv_b, ln_g, ln_b)


def _gla_kernel(q_ref, k_ref, v_ref, r_ref, a_ref, wup_ref, ba_ref, g_ref, o_ref,
                state, qe_s, qi_s, ki_s, kd_s, dec_s, obuf, *, rank, chunk):
    tb, dk = q_ref.shape
    dv = v_ref.shape[1]
    hk, hv = dk // GLA_HEADS, dv // GLA_HEADS
    nchunks = tb // chunk

    @pl.when(pl.program_id(0) == 0)
    def _():
        state[...] = jnp.zeros_like(state)

    lane = lax.broadcasted_iota(jnp.int32, a_ref.shape, 1)
    a_low = jnp.where(lane < rank, a_ref[...], 0.0)
    z = _dot_f32(a_low, wup_ref[...]) + ba_ref[...]
    la = (jnp.minimum(z, 0.0) - jnp.log(1.0 + jnp.exp(-jnp.abs(z)))) * (1.0 / GLA_GATE_TEMP)
    la_hi, la_lo, _ = _split3(la)

    rr = lax.broadcasted_iota(jnp.int32, (tb, tb), 0)
    cc = lax.broadcasted_iota(jnp.int32, (tb, tb), 1)
    sh = chunk.bit_length() - 1
    assert chunk == 1 << sh
    same = (rr >> sh) == (cc >> sh)
    tril = jnp.where(same & (cc <= rr), 1.0, 0.0).astype(BF16)
    ones_blk = jnp.where(same, 1.0, 0.0).astype(BF16)
    b = _dot(tril, la_hi) + _dot(tril, la_lo)
    bl = _dot(ones_blk, la_hi) + _dot(ones_blk, la_lo)

    sr = lax.broadcasted_iota(jnp.int32, (tb, nchunks * LANES), 0)
    sc = lax.broadcasted_iota(jnp.int32, (tb, nchunks * LANES), 1)
    sel = jnp.where((sr >> sh) == (sc >> (LANES.bit_length() - 1)), 1.0, 0.0).astype(BF16)
    tn_dims = (((0,), (0,)), ((), ()))
    tot_t = (lax.dot_general(la_hi, sel, tn_dims, preferred_element_type=F32)
             + lax.dot_general(la_lo, sel, tn_dims, preferred_element_type=F32))
    dec_s[...] = jnp.exp(tot_t)

    q = q_ref[...].astype(F32) * (float(hk) ** -0.5)
    k = k_ref[...].astype(F32)
    half = 0.5 * bl
    qe_s[...] = (q * jnp.exp(b)).astype(BF16)
    qi_s[...] = (q * jnp.exp(b - half)).astype(BF16)
    ki_s[...] = (k * jnp.exp(half - b)).astype(BF16)
    kd_s[...] = (k * jnp.exp(bl - b)).astype(BF16)

    ri = lax.broadcasted_iota(jnp.int32, (chunk, chunk), 0)
    ci = lax.broadcasted_iota(jnp.int32, (chunk, chunk), 1)
    causal = ci <= ri
    nt_dims = (((1,), (1,)), ((), ()))
    for c in range(nchunks):
        rows = pl.ds(c * chunk, chunk)
        for h in range(GLA_HEADS):
            ks = pl.ds(h * hk, hk)
            vs = pl.ds(h * hv, hv)
            vch = v_ref[rows, vs].astype(BF16)
            scores = lax.dot_general(qi_s[rows, ks], ki_s[rows, ks], nt_dims, preferred_element_type=F32)
            scores = jnp.where(causal, scores, 0.0).astype(BF16)
            s_old = state[h]
            o = _dot(qe_s[rows, ks], s_old.astype(BF16)) + _dot(scores, vch)
            obuf[rows, vs] = o
            dcol = dec_s[ks, pl.ds(c * LANES, LANES)]
            decay = jnp.concatenate([dcol] * (hv // LANES), axis=1) if hv >= LANES else dcol[:, :hv]
            state[h] = s_old * decay + lax.dot_general(kd_s[rows, ks], vch, tn_dims, preferred_element_type=F32)

    for h in range(GLA_HEADS):
        vs = pl.ds(h * hv, hv)
        o = obuf[:, vs]
        ms = jnp.mean(o * o, axis=-1, keepdims=True)
        on = o * lax.rsqrt(ms + LN_EPS) * g_ref[:, vs]
        r = r_ref[:, vs].astype(F32)
        o_ref[:, vs] = (on * (r * _sigmoid(r))).astype(o_ref.dtype)


def _gla(h2, a_low, wup_pad, b_a, norm_g, layer, *, dk, dv, rank):
    m = h2.shape[0]
    tb = _pick(m, 256, GLA_CHUNK)
    assert tb % GLA_CHUNK == 0 and dv == 2 * dk
    nchunks = tb // GLA_CHUNK
    hk, hv = dk // GLA_HEADS, dv // GLA_HEADS
    kernel = functools.partial(_gla_kernel, rank=rank, chunk=GLA_CHUNK)
    return pl.pallas_call(
        kernel, grid=(m // tb,),
        in_specs=[
            pl.BlockSpec((tb, dk), lambda i: (i, 0)),
            pl.BlockSpec((tb, dk), lambda i: (i, 1)),
            pl.BlockSpec((tb, dv), lambda i: (i, 1)),
            pl.BlockSpec((tb, dv), lambda i: (i, 2)),
            pl.BlockSpec((tb, LANES), lambda i: (i, 0)),
            pl.BlockSpec((None, LANES, dk), lambda i: (layer, 0, 0)),
            pl.BlockSpec((None, 1, dk), lambda i: (layer, 0, 0)),
            pl.BlockSpec((None, 1, dv), lambda i: (layer, 0, 0)),
        ],
        out_specs=pl.BlockSpec((tb, dv), lambda i: (i, 0)),
        out_shape=jax.ShapeDtypeStruct((m, dv), BF16),
        scratch_shapes=[
            pltpu.VMEM((GLA_HEADS, hk, hv), F32),
            pltpu.VMEM((tb, dk), BF16), pltpu.VMEM((tb, dk), BF16),
            pltpu.VMEM((tb, dk), BF16), pltpu.VMEM((tb, dk), BF16),
            pltpu.VMEM((dk, nchunks * LANES), F32),
            pltpu.VMEM((tb, dv), F32),
        ],
        name="gla",
        compiler_params=pltpu.CompilerParams(
            dimension_semantics=("arbitrary",), vmem_limit_bytes=_vmem_limit(48 << 20)),
    )(h2, h2, h2, h2, a_low, wup_pad, b_a, norm_g)


def _pack_halves(x):
    n = x.shape[1] // 2
    hi = lax.bitcast_convert_type(x[:, :n].astype(BF16).astype(F32), jnp.uint32)
    lo = lax.bitcast_convert_type(x[:, n:].astype(BF16).astype(F32), jnp.uint32)
    return hi | (lo >> 16)


def _unpack_halves(w):
    hi = lax.bitcast_convert_type(w & jnp.uint32(0xFFFF0000), F32)
    lo = lax.bitcast_convert_type(w << 16, F32)
    return hi, lo


def _ln_kernel(z_ref, g_ref, b_ref, *rest, with_router):
    if with_router:
        wr_ref, br_ref, x_ref, xp_ref, lg_ref = rest
    else:
        x_ref, *xb_refs = rest
    z = z_ref[...]
    mu = jnp.mean(z, axis=-1, keepdims=True)
    d = z - mu
    var = jnp.mean(d * d, axis=-1, keepdims=True)
    x = d * lax.rsqrt(var + LN_EPS) * g_ref[...] + b_ref[...]
    x_ref[...] = x
    if with_router:
        xp_ref[...] = _pack_halves(x)
        lg_ref[...] = _dot_f32(x, wr_ref[...]) + br_ref[...]
    for xb_ref in ([] if with_router else xb_refs):
        xb_ref[...] = x.astype(BF16)


def _layer_norm(z, ln_g, ln_b, idx, router=None, bf16_copy=True):
    m, d = z.shape
    tr = _pick(m, 512, 8)
    in_specs = [
        pl.BlockSpec((tr, d), lambda i: (i, 0)),
        pl.BlockSpec((None, 1, d), lambda i: (idx, 0, 0)),
        pl.BlockSpec((None, 1, d), lambda i: (idx, 0, 0)),
    ]
    args = [z, ln_g, ln_b]
    if router is None:
        out_specs = [pl.BlockSpec((tr, d), lambda i: (i, 0))] * (2 if bf16_copy else 1)
        out_shape = [jax.ShapeDtypeStruct((m, d), F32), jax.ShapeDtypeStruct((m, d), BF16)][:len(out_specs)]
    else:
        in_specs += [pl.BlockSpec((d, LANES), lambda i: (0, 0)), pl.BlockSpec((1, LANES), lambda i: (0, 0))]
        args += list(router)
        out_specs = [pl.BlockSpec((tr, d), lambda i: (i, 0)), pl.BlockSpec((tr, d // 2), lambda i: (i, 0)),
                     pl.BlockSpec((tr, LANES), lambda i: (i, 0))]
        out_shape = [jax.ShapeDtypeStruct((m, d), F32), jax.ShapeDtypeStruct((m, d // 2), jnp.uint32),
                     jax.ShapeDtypeStruct((m, LANES), F32)]
    return pl.pallas_call(
        functools.partial(_ln_kernel, with_router=router is not None),
        grid=(m // tr,), in_specs=in_specs, out_specs=out_specs, out_shape=out_shape,
        name="layer_norm_router" if router is not None else "layer_norm",
        compiler_params=pltpu.CompilerParams(
            dimension_semantics=("arbitrary",), vmem_limit_bytes=_vmem_limit(tr * d * 36)),
    )(*args)


def _route_kernel(lg_ref, idx_ref, w_ref, rank_ref, cnt_ref, *, n_experts):
    per = n_experts // N_GROUPS
    logit = [lg_ref[e] for e in range(n_experts)]
    mx = functools.reduce(jnp.maximum, logit)
    ex = [jnp.exp(l - mx) for l in logit]
    tot = functools.reduce(lambda a, c: a + c, ex)
    p = [e / tot for e in ex]

    def top2_sum(vals):
        a, bb, c, d = vals
        hi1, lo1 = jnp.maximum(a, bb), jnp.minimum(a, bb)
        hi2, lo2 = jnp.maximum(c, d), jnp.minimum(c, d)
        first = jnp.maximum(hi1, hi2)
        second = jnp.maximum(jnp.minimum(hi1, hi2), jnp.maximum(lo1, lo2))
        return first + second

    assert per == 4
    gs = [top2_sum(p[g * per:(g + 1) * per]) for g in range(N_GROUPS)]
    best = jnp.zeros_like(gs[0], dtype=jnp.int32)
    best_v = gs[0]
    for g in range(1, N_GROUPS):
        upd = gs[g] > best_v
        best = jnp.where(upd, g, best)
        best_v = jnp.where(upd, gs[g], best_v)
    vals = []
    for k in range(per):
        v = p[k]
        for g in range(1, N_GROUPS):
            v = jnp.where(best == g, p[g * per + k], v)
        vals.append(v)
    i1 = jnp.zeros_like(best)
    v1 = vals[0]
    for k in range(1, per):
        upd = vals[k] > v1
        i1 = jnp.where(upd, k, i1)
        v1 = jnp.where(upd, vals[k], v1)
    i2 = jnp.full_like(best, -1)
    v2 = jnp.full_like(v1, -1.0)
    for k in range(per):
        upd = (i1 != k) & (vals[k] > v2)
        i2 = jnp.where(upd, k, i2)
        v2 = jnp.where(upd, vals[k], v2)
    den = v1 + v2
    picks = (best * per + i1, best * per + i2)
    idx_ref[0] = picks[0]
    idx_ref[1] = picks[1]
    w_ref[0] = v1 / den
    w_ref[1] = v2 / den

    rows, lanes = best.shape
    li = lax.broadcasted_iota(jnp.int32, (lanes, lanes), 0)
    lj = lax.broadcasted_iota(jnp.int32, (lanes, lanes), 1)
    upper = jnp.where(li <= lj, 1.0, 0.0).astype(BF16)
    ones = jnp.ones((lanes, lanes), BF16)
    ri = lax.broadcasted_iota(jnp.int32, (rows, rows), 0)
    rj = lax.broadcasted_iota(jnp.int32, (rows, rows), 1)
    before = jnp.where(rj < ri, 1.0, 0.0).astype(BF16)
    ranks = [jnp.zeros((rows, lanes), F32), jnp.zeros((rows, lanes), F32)]
    for e in range(n_experts):
        base = jnp.zeros((1, lanes), F32)
        for s in range(2):
            hit = picks[s] == e
            ind = jnp.where(hit, 1.0, 0.0).astype(BF16)
            in_row = _dot(ind, upper)
            row_tot = _dot(ind, ones)
            row_off = _dot(before, row_tot.astype(BF16))
            ranks[s] = jnp.where(hit, in_row + row_off + (base - 1.0), ranks[s])
            base = base + (row_off + row_tot)[rows - 1:rows, :]
        cnt_ref[pl.ds(e, 1), :] = base.astype(jnp.int32)
    rank_ref[0] = ranks[0].astype(jnp.int32)
    rank_ref[1] = ranks[1].astype(jnp.int32)


def _route(logits_t):
    n_experts, rows, lanes = logits_t.shape
    return pl.pallas_call(
        functools.partial(_route_kernel, n_experts=n_experts),
        out_shape=[jax.ShapeDtypeStruct((2, rows, lanes), jnp.int32),
                   jax.ShapeDtypeStruct((2, rows, lanes), F32),
                   jax.ShapeDtypeStruct((2, rows, lanes), jnp.int32),
                   jax.ShapeDtypeStruct((n_experts, lanes), jnp.int32)],
        name="route",
    )(logits_t)


def _dispatch_kernel(pos_ref, tr_ref, x_ref, o_hbm, zeros, sem, zsem, *, m, tm):
    i = pl.program_id(0)
    chunk = x_ref.shape[0]
    part = zeros.shape[0]

    @pl.when(i == 0)
    def _():
        zeros[...] = jnp.zeros_like(zeros)
        for wait in (False, True):
            for tile in range(tr_ref.shape[0]):
                for q in range(tm // part):
                    @pl.when(tr_ref[tile] < (q + 1) * part)
                    def _(tile=tile, q=q, wait=wait):
                        fill = pltpu.make_async_copy(
                            zeros, o_hbm.at[pl.ds(tile * tm + q * part, part), :], zsem)
                        fill.wait() if wait else fill.start()

    def row_copy(r, dst):
        return pltpu.make_async_copy(x_ref.at[pl.ds(r, 1), :], o_hbm.at[pl.ds(dst, 1), :], sem)

    def issue(r, carry):
        row_copy(r, pos_ref[i * chunk + r]).start()
        row_copy(r, pos_ref[m + i * chunk + r]).start()
        return carry
    lax.fori_loop(0, chunk, issue, 0, unroll=8)

    def drain(r, carry):
        row_copy(r, 0).wait()
        row_copy(r, 0).wait()
        return carry
    lax.fori_loop(0, chunk, drain, 0, unroll=8)


def _dispatch_rows(xp, pos, tile_rows, *, tm):
    m, n = xp.shape
    chunk = _pick(m, 512, 8)
    return pl.pallas_call(
        functools.partial(_dispatch_kernel, m=m, tm=tm),
        grid_spec=pltpu.PrefetchScalarGridSpec(
            num_scalar_prefetch=2, grid=(m // chunk,),
            in_specs=[pl.BlockSpec((chunk, n), lambda i, pos, tr: (i, 0))],
            out_specs=pl.BlockSpec(memory_space=pl.ANY),
            scratch_shapes=[pltpu.VMEM((tm // ROW_PARTS, n), xp.dtype),
                            pltpu.SemaphoreType.DMA(()), pltpu.SemaphoreType.DMA(())]),
        out_shape=jax.ShapeDtypeStruct((tile_rows.shape[0] * tm, n), xp.dtype),
        name="expert_dispatch",
        compiler_params=pltpu.CompilerParams(dimension_semantics=("arbitrary",)),
    )(pos, tile_rows, xp)


def _stream_expert_weights(te_ref, nx_ref, live, w_hbm, stage, w_bf16, sem, layer):
    j, t = pl.program_id(0), pl.program_id(1)
    n_col_tiles = pl.num_programs(0)

    def fetch(k, e, jj):
        tw = stage[k].shape[1]
        src = w_hbm[k].at[layer, e, :, pl.ds(pl.multiple_of(jj * tw, tw), tw)]
        return pltpu.make_async_copy(src, stage[k], sem.at[k])

    @pl.when((j == 0) & (t == 0))
    def _():
        for k in range(len(stage)):
            fetch(k, te_ref[0], 0).start()

    @pl.when(live & ((t == 0) | (te_ref[t] != te_ref[jnp.maximum(t - 1, 0)])))
    def _():
        for k in range(len(stage)):
            fetch(k, te_ref[t], j).wait()
        rows = stage[0].shape[0]
        step = _pick(rows, 512, 16)

        def convert(c, carry):
            rs = pl.ds(pl.multiple_of(c * step, step), step)
            for k in range(len(stage)):
                w_bf16[k][rs, :] = stage[k][rs, :].astype(BF16)
            return carry
        lax.fori_loop(0, rows // step, convert, 0)
        nxt = nx_ref[t]

        @pl.when(nxt >= 0)
        def _():
            for k in range(len(stage)):
                fetch(k, nxt, j).start()

        @pl.when((nxt < 0) & (j + 1 < n_col_tiles))
        def _():
            for k in range(len(stage)):
                fetch(k, te_ref[0], j + 1).start()


ROW_PARTS = 4


def _compute_valid_rows(live, n_valid, compute, o_ref):
    tm = o_ref.shape[0]
    part = tm // ROW_PARTS
    for nr in range(part, tm + 1, part):
        @pl.when(live & (n_valid > nr - part) & (n_valid <= nr))
        def _(nr=nr):
            compute(nr)
            if nr < tm:
                o_ref[pl.ds(nr, tm - nr), :] = jnp.zeros((tm - nr, o_ref.shape[1]), o_ref.dtype)

    @pl.when(jnp.logical_not(live))
    def _():
        o_ref[...] = jnp.zeros_like(o_ref)


def _expert_up_kernel(te_ref, tr_ref, nt_ref, nx_ref, x_ref, wg_hbm, wu_hbm, o_ref,
                      wg_st, wu_st, wg_s, wu_s, sem, *, layer):
    t = pl.program_id(1)
    live = t < nt_ref[0]
    n_valid = tr_ref[t]
    half_d = x_ref.shape[1]
    _stream_expert_weights(te_ref, nx_ref, live, (wg_hbm, wu_hbm), (wg_st, wu_st), (wg_s, wu_s), sem, layer)

    def compute(nr):
        x_hi, x_lo = _unpack_halves(x_ref[pl.ds(0, nr), :])
        x_hi, x_lo = x_hi.astype(BF16), x_lo.astype(BF16)
        top, bot = pl.ds(0, half_d), pl.ds(half_d, half_d)
        g = _dot(x_hi, wg_s[top, :]) + _dot(x_lo, wg_s[bot, :])
        u = _dot(x_hi, wu_s[top, :]) + _dot(x_lo, wu_s[bot, :])
        o_ref[pl.ds(0, nr), :] = (g * _sigmoid(g) * u).astype(o_ref.dtype)

    _compute_valid_rows(live, n_valid, compute, o_ref)


def _live_tile(j, t, te, tr, nt, nx):
    return (jnp.minimum(t, nt[0] - 1), 0)


def _expert_up(xg, w_gate, w_up, layer, sched, *, tm):
    npad, half_d = xg.shape
    d = 2 * half_d
    f = w_gate.shape[-1]
    tf = _pick(f, 768)
    return pl.pallas_call(
        functools.partial(_expert_up_kernel, layer=layer),
        grid_spec=pltpu.PrefetchScalarGridSpec(
            num_scalar_prefetch=4, grid=(f // tf, npad // tm),
            in_specs=[pl.BlockSpec((tm, half_d), _live_tile),
                      pl.BlockSpec(memory_space=pl.ANY), pl.BlockSpec(memory_space=pl.ANY)],
            out_specs=pl.BlockSpec((tm, tf), lambda j, t, te, tr, nt, nx: (t, j)),
            scratch_shapes=[pltpu.VMEM((d, tf), F32), pltpu.VMEM((d, tf), F32),
                            pltpu.VMEM((d, tf), BF16), pltpu.VMEM((d, tf), BF16),
                            pltpu.SemaphoreType.DMA((2,))]),
        out_shape=jax.ShapeDtypeStruct((npad, f), BF16),
        name="expert_up",
        compiler_params=pltpu.CompilerParams(
            dimension_semantics=("arbitrary", "arbitrary"),
            vmem_limit_bytes=_vmem_limit(2 * d * tf * 4 + 2 * d * tf * 2 + 5 * tm * d * 2 + 8 * tm * tf * 4)),
    )(*sched, xg, w_gate, w_up)


def _expert_down_kernel(te_ref, tr_ref, nt_ref, nx_ref, h_ref, wd_hbm, o_ref, wd_st, wd_s, sem, *, layer):
    t = pl.program_id(1)
    live = t < nt_ref[0]
    n_valid = tr_ref[t]
    _stream_expert_weights(te_ref, nx_ref, live, (wd_hbm,), (wd_st,), (wd_s,), sem, layer)

    def compute(nr):
        o_ref[pl.ds(0, nr), :] = _pack_halves(_dot(h_ref[pl.ds(0, nr), :], wd_s[...]))

    _compute_valid_rows(live, n_valid, compute, o_ref)


def _expert_down(hg, w_down, layer, sched, *, tm, tn):
    npad, f = hg.shape
    d = w_down.shape[-1]
    return pl.pallas_call(
        functools.partial(_expert_down_kernel, layer=layer),
        grid_spec=pltpu.PrefetchScalarGridSpec(
            num_scalar_prefetch=4, grid=(d // tn, npad // tm),
            in_specs=[pl.BlockSpec((tm, f), _live_tile), pl.BlockSpec(memory_space=pl.ANY)],
            out_specs=pl.BlockSpec((tm, tn // 2), lambda j, t, te, tr, nt, nx: (t, j)),
            scratch_shapes=[pltpu.VMEM((f, tn), F32), pltpu.VMEM((f, tn), BF16),
                            pltpu.SemaphoreType.DMA((1,))]),
        out_shape=jax.ShapeDtypeStruct((npad, d // 2), jnp.uint32),
        name="expert_down",
        compiler_params=pltpu.CompilerParams(
            dimension_semantics=("arbitrary", "arbitrary"),
            vmem_limit_bytes=_vmem_limit(f * tn * 4 + f * tn * 2 + 2 * tm * f * 2 + 6 * tm * tn * 4)),
    )(*sched, hg, w_down)


def _combine_ln_kernel(pos_ref, y_hbm, x_ref, w0_ref, w1_ref, g_ref, b_ref, o_ref, ob_ref, buf, sem,
                       *, alpha, m, tn):
    i = pl.program_id(0)
    tc, d = x_ref.shape

    def row_copy(slot, s, r, p):
        return pltpu.make_async_copy(y_hbm.at[pl.ds(p, 1), :], buf.at[slot, s, pl.ds(r, 1), :], sem.at[slot])

    def gather(step):
        def issue(r, carry):
            row_copy(step % 2, 0, r, pos_ref[step * tc + r]).start()
            row_copy(step % 2, 1, r, pos_ref[m + step * tc + r]).start()
            return carry
        lax.fori_loop(0, tc, issue, 0, unroll=8)

    @pl.when(i == 0)
    def _():
        gather(i)

    @pl.when(i + 1 < pl.num_programs(0))
    def _():
        gather(i + 1)

    def drain(r, carry):
        row_copy(i % 2, 0, r, 0).wait()
        row_copy(i % 2, 1, r, 0).wait()
        return carry
    lax.fori_loop(0, tc, drain, 0, unroll=8)

    w0, w1 = w0_ref[...], w1_ref[...]
    rows = buf.at[i % 2]
    parts = []
    for j in range(d // tn):
        cs = pl.ds(j * (tn // 2), tn // 2)
        hi0, lo0 = _unpack_halves(rows[0, :, cs])
        hi1, lo1 = _unpack_halves(rows[1, :, cs])
        parts += [w0 * hi0 + w1 * hi1, w0 * lo0 + w1 * lo1]
    z = alpha * x_ref[...] + jnp.concatenate(parts, axis=1)
    mu = jnp.mean(z, axis=-1, keepdims=True)
    dz = z - mu
    var = jnp.mean(dz * dz, axis=-1, keepdims=True)
    x = dz * lax.rsqrt(var + LN_EPS) * g_ref[...] + b_ref[...]
    o_ref[...] = x
    ob_ref[...] = x.astype(BF16)


def _combine_ln(yg, pos, ewts, x, ln_g, ln_b, idx, *, alpha, tn):
    m, d = x.shape
    tc = _pick(m, 256, 8)
    row = lambda i, pos: (i, 0)
    par = lambda i, pos: (idx, 0, 0)
    return pl.pallas_call(
        functools.partial(_combine_ln_kernel, alpha=alpha, m=m, tn=tn),
        grid_spec=pltpu.PrefetchScalarGridSpec(
            num_scalar_prefetch=1, grid=(m // tc,),
            in_specs=[pl.BlockSpec(memory_space=pl.ANY),
                      pl.BlockSpec((tc, d), row),
                      pl.BlockSpec((None, tc, 1), lambda i, pos: (0, i, 0)),
                      pl.BlockSpec((None, tc, 1), lambda i, pos: (1, i, 0)),
                      pl.BlockSpec((None, 1, d), par), pl.BlockSpec((None, 1, d), par)],
            out_specs=[pl.BlockSpec((tc, d), row), pl.BlockSpec((tc, d), row)],
            scratch_shapes=[pltpu.VMEM((2, 2, tc, d // 2), jnp.uint32), pltpu.SemaphoreType.DMA((2,))]),
        out_shape=[jax.ShapeDtypeStruct((m, d), F32), jax.ShapeDtypeStruct((m, d), BF16)],
        name="combine_ln",
        compiler_params=pltpu.CompilerParams(
            dimension_semantics=("arbitrary",), vmem_limit_bytes=_vmem_limit(tc * d * 48)),
    )(pos, yg, x, ewts, ewts, ln_g, ln_b)


def _expert_schedule(eidx, rank, counts, tm, n_tiles_max):
    n_experts = counts.shape[0]
    tiles = (counts + tm - 1) // tm
    tile_end = jnp.cumsum(tiles)
    tile_start = tile_end - tiles
    experts = jnp.arange(n_experts, dtype=jnp.int32)
    pick = eidx[:, :, None] == experts
    pos = rank + jnp.sum(jnp.where(pick, tile_start * tm, 0), axis=-1)
    n_tiles = tile_end[-1]
    tile_ids = jnp.arange(n_tiles_max, dtype=jnp.int32)
    tile_e = jnp.sum(tile_end[None, :] <= jnp.minimum(tile_ids, n_tiles - 1)[:, None], axis=-1)
    mine = tile_e[:, None] == experts
    filled = jnp.sum(jnp.where(mine, counts - (tile_ids[:, None] - tile_start) * tm, 0), axis=-1)
    tile_rows = jnp.clip(filled, 0, tm)
    after = jnp.sum(jnp.where(mine, tile_end, 0), axis=-1)
    expert_after = jnp.sum(tile_end[None, :] <= after[:, None], axis=-1)
    next_e = jnp.where(after < n_tiles, expert_after, -1)
    i32 = lambda a: a.astype(jnp.int32)
    return i32(pos).reshape(-1), (i32(tile_e), i32(tile_rows), i32(n_tiles).reshape(1), i32(next_e))


def kernel(x, p, w_in, b_branch_gate, w_gla_gate_up, b_gla_gate, conv_w, conv_b, conv_ln_g, conv_ln_b, w_conv_out, gla_norm_g, w_gla_out, w_out, ln_g, ln_b, w_router, b_router, w_exp_gate, w_exp_up, w_exp_down, w_ple, w_ple_gate):
    bsz, seq, d = x.shape
    assert bsz == 1
    depth = w_in.shape[0]
    m = bsz * seq
    conv_ch = conv_w.shape[-1]
    rank, dk = w_gla_gate_up.shape[1], w_gla_gate_up.shape[2]
    dv = gla_norm_g.shape[-1]
    n_experts = w_router.shape[1]
    off_b = conv_ch
    off_q = 2 * conv_ch
    off_alpha = off_q + 2 * dk + 2 * dv
    off_gate = off_alpha + rank
    alpha = (2.0 * depth) ** 0.25
    tm_e = 512 if m >= 4096 else 64
    n_tiles_max = (2 * m) // tm_e + n_experts
    tn_down = _pick(d, 2048)

    xf = x.reshape(m, d)
    xb = xf.astype(BF16)
    pb = p.reshape(depth, m, p.shape[-1]).astype(BF16)
    w_in_t = jnp.swapaxes(w_in, 1, 2)
    b_gate = b_branch_gate.reshape(depth, 1, 2 * d)
    wup_pad = jnp.zeros((depth, LANES, dk), F32).at[:, :rank, :].set(w_gla_gate_up)
    b_a = b_gla_gate.reshape(depth, 1, dk)
    conv_b3 = conv_b.reshape(depth, 1, conv_ch)
    cln_g = conv_ln_g.reshape(depth, 1, conv_ch)
    cln_b = conv_ln_b.reshape(depth, 1, conv_ch)
    gnorm = gla_norm_g.reshape(depth, 1, dv)
    ln_g3 = ln_g.reshape(depth * 3, 1, d)
    ln_b3 = ln_b.reshape(depth * 3, 1, d)
    wr_pad = jnp.zeros((d, LANES), F32).at[:, :n_experts].set(w_router)
    br_pad = jnp.zeros((1, LANES), F32).at[0, :n_experts].set(b_router)

    tm = _pick(m, 1024, 8)
    for i in range(depth):
        (u,) = _fused_matmul(
            [xb], [(0, w_in_t, i, 0, True), (0, w_in_t, i, off_b, True)], [],
            lambda ds, ex: [ds[0] * _sigmoid(ds[1])], [F32],
            n_cols=conv_ch, tm=tm, tn=_pick(conv_ch, 256), name="in_proj_glu")
        (h2,) = _fused_matmul(
            [xb], [(0, w_in_t, i, off_q, True)], [], lambda ds, ex: [ds[0]], [F32],
            n_cols=2 * dk + 2 * dv, tm=tm, tn=_pick(dk, 512), name="in_proj_qkvr")
        (a_low,) = _fused_matmul(
            [xb], [(0, w_in_t, i, off_alpha, True)], [], lambda ds, ex: [ds[0]], [F32],
            n_cols=LANES, tm=tm, tn=LANES, name="in_proj_alpha")
        (gates,) = _fused_matmul(
            [xb], [(0, w_in_t, i, off_gate, True)], [('row', b_gate, i, 0)],
            lambda ds, ex: [_sigmoid(ds[0] + ex[0])], [BF16],
            n_cols=2 * d, tm=tm, tn=_pick(d, 512), name="in_proj_gates")

        uc = _conv_ln_silu(u, conv_w, conv_b3, cln_g, cln_b, i)
        og = _gla(h2, a_low, wup_pad, b_a, gnorm, i, dk=dk, dv=dv, rank=rank)

        tn_o = _pick(d, 512)
        (merged,) = _fused_matmul(
            [uc, og], [(0, w_conv_out, i, 0, False), (1, w_gla_out, i, 0, False)],
            [('tile', gates, None, 0), ('tile', gates, None, d)],
            lambda ds, ex: [ex[0].astype(F32) * ds[0] + ex[1].astype(F32) * ds[1]], [BF16],
            n_cols=d, tm=tm, tn=tn_o, name="branch_out_merge")
        (z1,) = _fused_matmul(
            [merged], [(0, w_out, i, 0, False)], [('tile', xf, None, 0)],
            lambda ds, ex: [alpha * ex[0] + ds[0]], [F32],
            n_cols=d, tm=tm, tn=tn_o, name="mixer_out")
        x1, x1p, logits = _layer_norm(z1, ln_g3, ln_b3, 3 * i, router=(wr_pad, br_pad))

        logits_t = logits[:, :n_experts].T.reshape(n_experts, m // LANES, LANES)
        eidx, ewts, pair_rank, counts = _route(logits_t)
        pos, sched = _expert_schedule(
            eidx.reshape(2, m), pair_rank.reshape(2, m), counts[:, 0], tm_e, n_tiles_max)
        xg = _dispatch_rows(x1p, pos, sched[1], tm=tm_e)
        hg = _expert_up(xg, w_exp_gate, w_exp_up, i, sched, tm=tm_e)
        yg = _expert_down(hg, w_exp_down, i, sched, tm=tm_e, tn=tn_down)
        x2, x2b = _combine_ln(yg, pos, ewts.reshape(2, m, 1), x1, ln_g3, ln_b3, 3 * i + 1,
                              alpha=alpha, tn=tn_down)

        (z3,) = _fused_matmul(
            [x2b, pb[i]], [(0, w_ple_gate, i, 0, False), (1, w_ple, i, 0, False)], [('tile', x2, None, 0)],
            lambda ds, ex: [alpha * ex[0] + _sigmoid(ds[0]) * ds[1]], [F32],
            n_cols=d, tm=tm, tn=tn_o, name="ple")
        xf, *xb = _layer_norm(z3, ln_g3, ln_b3, 3 * i + 2, bf16_copy=i + 1 < depth)
        xb = xb[0] if xb else None

    return xf.reshape(bsz, seq, d)
```
